```python
import math
import jax, jax.numpy as jnp
from jax import lax
import numpy as np

D_MODEL = 1024
BATCH = 16
SEQ = 2048
DEPTH = 4

N_MIXERS = 3
MLA_HEADS = 8
MLA_Q_LORA = 384
MLA_KV_LORA = 256
MLA_NOPE = 128
MLA_ROPE = 64
MLA_V = 128
MLA_QK = MLA_NOPE + MLA_ROPE
ROPE_THETA = 10000.0
Q_BLOCK = 128
HG_HEADS = 8
HG_DK = D_MODEL // HG_HEADS
HG_DV = D_MODEL // HG_HEADS
HG_CHUNK = 64
CONV_WIDTH = 31
D_FF = 4 * D_MODEL
EPS = 1e-6

N_MLA_LAYERS = (DEPTH + 2) // 3
N_HGRN_LAYERS = (DEPTH + 1) // 3
N_CONV_LAYERS = DEPTH // 3

kernel_name = "hybrid_mla_hgrn2_conformer_trunk"


def rms_norm(x, g):
    xf = x.astype(jnp.float32)
    y = xf * lax.rsqrt(jnp.mean(xf * xf, axis=-1, keepdims=True) + EPS)
    return (y * g.astype(jnp.float32)).astype(x.dtype)


def layer_norm(x, g, b):
    xf = x.astype(jnp.float32)
    mu = jnp.mean(xf, axis=-1, keepdims=True)
    xc = xf - mu
    y = xc * lax.rsqrt(jnp.mean(xc * xc, axis=-1, keepdims=True) + EPS)
    return (y * g.astype(jnp.float32) + b.astype(jnp.float32)).astype(x.dtype)


def apply_rope(t, cos, sin):
    t1, t2 = jnp.split(t.astype(jnp.float32), 2, axis=-1)
    out = jnp.concatenate([t1 * cos - t2 * sin, t2 * cos + t1 * sin], axis=-1)
    return out.astype(t.dtype)


def causal_block_attention(q, k, v):
    B, S, H, Dq = q.shape
    nb = S // Q_BLOCK
    scale = Dq ** -0.5
    qb = q.reshape(B, nb, Q_BLOCK, H, Dq).transpose(1, 0, 2, 3, 4)
    key_pos = jnp.arange(S)

    def one_block(args):
        q_blk, blk = args
        q_pos = blk * Q_BLOCK + jnp.arange(Q_BLOCK)
        s = jnp.einsum('bqhd,bkhd->bhqk', q_blk, k, preferred_element_type=jnp.float32) * scale
        s = jnp.where(key_pos[None, :] <= q_pos[:, None], s, -jnp.inf)
        p = jax.nn.softmax(s, axis=-1)
        return jnp.einsum('bhqk,bkhd->bqhd', p.astype(v.dtype), v)

    ob = lax.map(one_block, (qb, jnp.arange(nb)))
    return ob.transpose(1, 0, 2, 3, 4).reshape(B, S, H, v.shape[-1])


def mla_mixer(h, cos, sin, w_down, q_lat_norm, kv_lat_norm, w_uq, w_ukv, q_head_norm, k_head_norm, w_o):
    B, S, _ = h.shape
    lat = h @ w_down
    c_q, c_kv, k_rope = jnp.split(lat, [MLA_Q_LORA, MLA_Q_LORA + MLA_KV_LORA], axis=-1)
    c_q = rms_norm(c_q, q_lat_norm)
    c_kv = rms_norm(c_kv, kv_lat_norm)
    q = (c_q @ w_uq).reshape(B, S, MLA_HEADS, MLA_QK)
    kv = (c_kv @ w_ukv).reshape(B, S, MLA_HEADS, MLA_NOPE + MLA_V)
    q_nope, q_rope = jnp.split(q, [MLA_NOPE], axis=-1)
    k_nope, v = jnp.split(kv, [MLA_NOPE], axis=-1)
    q_nope = rms_norm(q_nope, q_head_norm[:MLA_NOPE])
    q_rope = rms_norm(q_rope, q_head_norm[MLA_NOPE:])
    k_nope = rms_norm(k_nope, k_head_norm[:MLA_NOPE])
    k_rope = rms_norm(k_rope, k_head_norm[MLA_NOPE:])
    q_rope = apply_rope(q_rope, cos[:, :, None, :], sin[:, :, None, :])
    k_rope = apply_rope(k_rope, cos, sin)
    q = jnp.concatenate([q_nope, q_rope], axis=-1)
    k = jnp.concatenate([k_nope, jnp.broadcast_to(k_rope[:, :, None, :], (B, S, MLA_HEADS, MLA_ROPE))], axis=-1)
    o = causal_block_attention(q, k, v)
    return o.reshape(B, S, MLA_HEADS * MLA_V) @ w_o


def hgrn2_chunk_scan(q, k, v, log_f):
    _, B, H, C, dk = q.shape
    dv = v.shape[-1]
    causal = jnp.tril(jnp.ones((C, C), dtype=bool))

    def step(state, inp):
        qc, kc, vc, lfc = inp
        b = jnp.cumsum(lfc, axis=2)
        b_last = b[:, :, -1:, :]
        o_inter = jnp.einsum('bhtk,bhkv->bhtv', qc * jnp.exp(b), state)
        diff = b[:, :, :, None, :] - b[:, :, None, :, :]
        decay = jnp.exp(jnp.where(causal[:, :, None], diff, -jnp.inf))
        attn = jnp.einsum('bhtk,bhsk,bhtsk->bhts', qc, kc, decay)
        o_intra = jnp.einsum('bhts,bhsv->bhtv', attn, vc)
        new_state = jnp.exp(b_last[:, :, 0, :])[..., None] * state + jnp.einsum(
            'bhsk,bhsv->bhkv', kc * jnp.exp(b_last - b), vc)
        return new_state, o_inter + o_intra

    init = jnp.zeros((B, H, dk, dv), jnp.float32)
    _, o = lax.scan(step, init, (q, k, v, log_f))
    return o


def hgrn2_mixer(h, lb, w_in, out_norm, w_o):
    B, S, _ = h.shape
    nC = S // HG_CHUNK
    proj = h @ w_in
    q, fz, i, g = jnp.split(proj, 4, axis=-1)
    fz = fz.astype(jnp.float32)
    f = lb + (1.0 - lb) * jax.nn.sigmoid(fz)
    log_f = jnp.log(f)
    k = (1.0 - lb) * jax.nn.sigmoid(-fz)

    def to_chunks(t, d):
        return t.astype(jnp.float32).reshape(B, nC, HG_CHUNK, HG_HEADS, d).transpose(1, 0, 3, 2, 4)

    o = hgrn2_chunk_scan(to_chunks(q, HG_DK), to_chunks(k, HG_DK), to_chunks(i, HG_DV), to_chunks(log_f, HG_DK))
    o = o.transpose(1, 0, 3, 2, 4).reshape(B, S, HG_HEADS, HG_DV)
    gate = jax.nn.silu(g.astype(jnp.float32)).reshape(B, S, HG_HEADS, HG_DV)
    o = rms_norm(o, out_norm) * gate
    return o.reshape(B, S, HG_HEADS * HG_DV).astype(h.dtype) @ w_o


def conformer_conv_mixer(h, w_pw1, b_pw1, w_dw, b_dw, ln_g, ln_b, w_pw2, b_pw2):
    a, gate = jnp.split(h @ w_pw1 + b_pw1, 2, axis=-1)
    u = a * jax.nn.sigmoid(gate)
    u = lax.conv_general_dilated(
        u, w_dw[:, None, :].astype(u.dtype), window_strides=(1,), padding=[(CONV_WIDTH - 1, 0)],
        dimension_numbers=('NWC', 'WIO', 'NWC'), feature_group_count=D_MODEL) + b_dw
    u = jax.nn.silu(layer_norm(u, ln_g, ln_b))
    return u @ w_pw2 + b_pw2


def squared_relu_mlp(h, w_in, w_out):
    a = jax.nn.relu(h @ w_in)
    return (a * a) @ w_out


def setup_inputs(seed: int = 0) -> dict:
    key = jax.random.key(seed)
    ks = iter(jax.random.split(key, 32))
    f32 = jnp.float32

    def nrm(shape, scale):
        return jax.random.normal(next(ks), shape, f32) * scale

    def gain(shape):
        return 1.0 + 0.1 * jax.random.normal(next(ks), shape, f32)

    def bias(shape):
        return 0.02 * jax.random.normal(next(ks), shape, f32)

    x = nrm((BATCH, SEQ, D_MODEL), 1.0)
    offsets = jax.random.randint(next(ks), (BATCH, 1), 0, 4096, dtype=jnp.int32)
    positions = offsets + jnp.arange(SEQ, dtype=jnp.int32)[None, :]
    nA, nB, nC = N_MLA_LAYERS, N_HGRN_LAYERS, N_CONV_LAYERS
    return {
        "x": x,
        "positions": positions,
        "norm_mix": gain((DEPTH, D_MODEL)),
        "norm_mlp": gain((DEPTH, D_MODEL)),
        "mlp_w_in": nrm((DEPTH, D_MODEL, D_FF), D_MODEL ** -0.5),
        "mlp_w_out": nrm((DEPTH, D_FF, D_MODEL), D_FF ** -0.5),
        "mla_w_down": nrm((nA, D_MODEL, MLA_Q_LORA + MLA_KV_LORA + MLA_ROPE), D_MODEL ** -0.5),
        "mla_q_lat_norm": gain((nA, MLA_Q_LORA)),
        "mla_kv_lat_norm": gain((nA, MLA_KV_LORA)),
        "mla_w_uq": nrm((nA, MLA_Q_LORA, MLA_HEADS * MLA_QK), MLA_Q_LORA ** -0.5),
        "mla_w_ukv": nrm((nA, MLA_KV_LORA, MLA_HEADS * (MLA_NOPE + MLA_V)), MLA_KV_LORA ** -0.5),
        "mla_q_head_norm": gain((nA, MLA_QK)),
        "mla_k_head_norm": gain((nA, MLA_QK)),
        "mla_w_o": nrm((nA, MLA_HEADS * MLA_V, D_MODEL), (MLA_HEADS * MLA_V) ** -0.5),
        "hg_w_in": nrm((nB, D_MODEL, 2 * HG_HEADS * HG_DK + 2 * HG_HEADS * HG_DV), D_MODEL ** -0.5),
        "hg_lb_logits": nrm((DEPTH, HG_HEADS * HG_DK), 0.5),
        "hg_out_norm": gain((nB, HG_DV)),
        "hg_w_o": nrm((nB, HG_HEADS * HG_DV, D_MODEL), (HG_HEADS * HG_DV) ** -0.5),
        "cv_w_pw1": nrm((nC, D_MODEL, 2 * D_MODEL), D_MODEL ** -0.5),
        "cv_b_pw1": bias((nC, 2 * D_MODEL)),
        "cv_w_dw": nrm((nC, CONV_WIDTH, D_MODEL), CONV_WIDTH ** -0.5),
        "cv_b_dw": bias((nC, D_MODEL)),
        "cv_ln_g": gain((nC, D_MODEL)),
        "cv_ln_b": bias((nC, D_MODEL)),
        "cv_w_pw2": nrm((nC, D_MODEL, D_MODEL), D_MODEL ** -0.5),
        "cv_b_pw2": bias((nC, D_MODEL)),
    }


def reference(x, positions, norm_mix, norm_mlp, mlp_w_in, mlp_w_out,
              mla_w_down, mla_q_lat_norm, mla_kv_lat_norm, mla_w_uq, mla_w_ukv,
              mla_q_head_norm, mla_k_head_norm, mla_w_o,
              hg_w_in, hg_lb_logits, hg_out_norm, hg_w_o,
              cv_w_pw1, cv_b_pw1, cv_w_dw, cv_b_dw, cv_ln_g, cv_ln_b, cv_w_pw2, cv_b_pw2):
    f32 = jnp.float32
    inv_freq = jnp.power(ROPE_THETA, -jnp.arange(0, MLA_ROPE, 2, dtype=f32) / MLA_ROPE)
    ang = positions.astype(f32)[..., None] * inv_freq
    cos, sin = jnp.cos(ang), jnp.sin(ang)
    lb_table = jnp.cumsum(jax.nn.softmax(hg_lb_logits.astype(f32), axis=0), axis=0)
    lb_table = lb_table - lb_table[0]

    for layer in range(DEPTH):
        kind, j = layer % N_MIXERS, layer // N_MIXERS
        h = rms_norm(x, norm_mix[layer])
        if kind == 0:
            mix = mla_mixer(h, cos, sin, mla_w_down[j], mla_q_lat_norm[j], mla_kv_lat_norm[j],
                            mla_w_uq[j], mla_w_ukv[j], mla_q_head_norm[j], mla_k_head_norm[j], mla_w_o[j])
        elif kind == 1:
            mix = hgrn2_mixer(h, lb_table[layer], hg_w_in[j], hg_out_norm[j], hg_w_o[j])
        else:
            mix = conformer_conv_mixer(h, cv_w_pw1[j], cv_b_pw1[j], cv_w_dw[j], cv_b_dw[j],
                                       cv_ln_g[j], cv_ln_b[j], cv_w_pw2[j], cv_b_pw2[j])
        x = x + mix.astype(x.dtype)
        h = rms_norm(x, norm_mlp[layer])
        x = x + squared_relu_mlp(h, mlp_w_in[layer], mlp_w_out[layer]).astype(x.dtype)
    return x
```

```python
import functools

import numpy as np
import jax
import jax.numpy as jnp
from jax import lax
from jax.experimental import pallas as pl
from jax.experimental.pallas import tpu as pltpu

F32 = jnp.float32
BF16 = jnp.bfloat16

D_MODEL = 1024
N_MIXERS = 3
MLA_HEADS = 8
MLA_Q_LORA = 384
MLA_KV_LORA = 256
MLA_NOPE = 128
MLA_ROPE = 64
MLA_V = 128
MLA_QK = MLA_NOPE + MLA_ROPE
ROPE_THETA = 10000.0
HG_HEADS = 8
HG_D = D_MODEL // HG_HEADS
HG_CHUNK = 64
CONV_WIDTH = 31
D_FF = 4 * D_MODEL
EPS = 1e-6

LANES = 128
HEAD_PAD = 2 * LANES
VMEM_LIMIT = 56 * 1024 * 1024

TM_PROJ = 512
TM_TAIL = 512
TF_TAIL = 512
TQ_ATTN = 256
TM_CONV = 512
CONV_HALO = 32


def _cparams(*sem):
    return pltpu.CompilerParams(dimension_semantics=sem, vmem_limit_bytes=VMEM_LIMIT)


def _rms(x, g):
    return x * lax.rsqrt(jnp.mean(x * x, axis=-1, keepdims=True) + EPS) * g


def _dot(a, b):
    return jnp.dot(a, b, preferred_element_type=F32)


def _dot_nt(a, b):
    return lax.dot_general(a, b, (((1,), (1,)), ((), ())), preferred_element_type=F32)


def _row_spec(tm, n):
    return pl.BlockSpec((tm, n), lambda i: (i, 0))


def _const_spec(shape):
    return pl.BlockSpec(shape, lambda *_: (0,) * len(shape))


def _resident_spec(shape):
    return pl.BlockSpec(shape, lambda *_: (0,) * len(shape), pipeline_mode=pl.Buffered(1))


def _rope_table_kernel(pos_ref, invf_ref, cos_ref, sin_ref):
    ang = pos_ref[...] * invf_ref[...]
    lane = lax.broadcasted_iota(jnp.int32, (1, LANES), 1)
    sign = jnp.where((lane // (MLA_ROPE // 2)) % 2 == 0, -1.0, 1.0)
    cos_ref[...] = jnp.cos(ang)
    sin_ref[...] = jnp.sin(ang) * sign


def _rope_tables(pos_col, invf4):
    T = pos_col.shape[0]
    tm = TM_PROJ
    return pl.pallas_call(
        _rope_table_kernel,
        grid=(T // tm,),
        in_specs=[_row_spec(tm, 1), _const_spec((1, LANES))],
        out_specs=[_row_spec(tm, LANES), _row_spec(tm, LANES)],
        out_shape=[jax.ShapeDtypeStruct((T, LANES), F32)] * 2,
        compiler_params=_cparams("parallel"),
        name="rope_tables",
    )(pos_col, invf4)


def _mla_proj_kernel(x_ref, g_ref, wd_ref, qlg_ref, kvlg_ref, wuq_ref, wukv_ref,
                     qng_ref, qrg_ref, kng_ref, krg_ref, cos_ref, sin_ref,
                     q_ref, kn_ref, kr_ref, v_ref):
    h = _rms(x_ref[...], g_ref[...]).astype(BF16)
    lat = _dot(h, wd_ref[...])
    c_q = _rms(lat[:, :MLA_Q_LORA], qlg_ref[...]).astype(BF16)
    c_kv = _rms(lat[:, MLA_Q_LORA:MLA_Q_LORA + MLA_KV_LORA], kvlg_ref[...]).astype(BF16)
    cos = cos_ref[...]
    sin = sin_ref[...]

    def rope(r):
        return r * cos + pltpu.roll(r, MLA_ROPE // 2, 1) * sin

    lane = lax.broadcasted_iota(jnp.int32, (1, LANES), 1)
    kr = rope(_rms(lat[:, MLA_Q_LORA + MLA_KV_LORA:], krg_ref[...]))
    kr_ref[...] = jnp.where(lane < MLA_ROPE, kr, 0.0).astype(BF16)

    scale = MLA_QK ** -0.5
    qng = qng_ref[...] * scale
    qrg = qrg_ref[...] * scale
    kng = kng_ref[...]
    for hh in range(MLA_HEADS):
        q = _dot(c_q, wuq_ref[:, hh * HEAD_PAD:(hh + 1) * HEAD_PAD])
        q_ref[:, hh * HEAD_PAD:hh * HEAD_PAD + LANES] = _rms(q[:, :LANES], qng).astype(BF16)
        q_ref[:, hh * HEAD_PAD + LANES:(hh + 1) * HEAD_PAD] = rope(_rms(q[:, LANES:], qrg)).astype(BF16)
        kv = _dot(c_kv, wukv_ref[:, hh * 2 * LANES:(hh + 1) * 2 * LANES])
        kn_ref[:, hh * LANES:(hh + 1) * LANES] = _rms(kv[:, :LANES], kng).astype(BF16)
        v_ref[:, hh * LANES:(hh + 1) * LANES] = kv[:, LANES:].astype(BF16)


def _mla_proj(x2, g, wd, qlg, kvlg, wuq, wukv, qng, qrg, kng, krg, cos, sin):
    T = x2.shape[0]
    tm = TM_PROJ
    n_lat = wd.shape[1]
    return pl.pallas_call(
        _mla_proj_kernel,
        grid=(T // tm,),
        in_specs=[_row_spec(tm, D_MODEL), _const_spec((1, D_MODEL)), _resident_spec(wd.shape),
                  _const_spec((1, MLA_Q_LORA)), _const_spec((1, MLA_KV_LORA)),
                  _resident_spec(wuq.shape), _resident_spec(wukv.shape),
                  _const_spec((1, LANES)), _const_spec((1, LANES)), _const_spec((1, LANES)),
                  _const_spec((1, LANES)), _row_spec(tm, LANES), _row_spec(tm, LANES)],
        out_specs=[_row_spec(tm, MLA_HEADS * HEAD_PAD), _row_spec(tm, MLA_HEADS * LANES),
                   _row_spec(tm, LANES), _row_spec(tm, MLA_HEADS * LANES)],
        out_shape=[jax.ShapeDtypeStruct((T, MLA_HEADS * HEAD_PAD), BF16),
                   jax.ShapeDtypeStruct((T, MLA_HEADS * LANES), BF16),
                   jax.ShapeDtypeStruct((T, LANES), BF16),
                   jax.ShapeDtypeStruct((T, MLA_HEADS * LANES), BF16)],
        compiler_params=_cparams("parallel"),
        name="mla_proj",
    )(x2, g, wd, qlg, kvlg, wuq, wukv, qng, qrg, kng, krg, cos, sin)


def _attn_kernel(q_ref, kn_ref, kr_ref, v_ref, o_ref, kcat_ref, *, tq):
    qi = pl.program_id(2)

    @pl.when(qi == 0)
    def _():
        kcat_ref[:, :LANES] = kn_ref[0]
        kcat_ref[:, LANES:] = kr_ref[0]

    q = q_ref[0]

    def block(j, carry, masked):
        m, l, acc = carry
        start = pl.multiple_of(j * tq, tq)
        s = _dot_nt(q, kcat_ref[pl.ds(start, tq), :])
        if masked:
            row = lax.broadcasted_iota(jnp.int32, (tq, tq), 0)
            col = lax.broadcasted_iota(jnp.int32, (tq, tq), 1)
            s = jnp.where(col <= row, s, -jnp.inf)
        m_new = jnp.maximum(m, jnp.max(s, axis=-1, keepdims=True))
        alpha = jnp.exp(m - m_new)
        p = jnp.exp(s - m_new)
        l = alpha * l + jnp.sum(p, axis=-1, keepdims=True)
        acc = alpha * acc + _dot(p.astype(BF16), v_ref[0, pl.ds(start, tq), :])
        return m_new, l, acc

    init = (jnp.full((tq, 1), -jnp.inf, F32), jnp.zeros((tq, 1), F32), jnp.zeros((tq, MLA_V), F32))
    carry = lax.fori_loop(0, qi, lambda j, c: block(j, c, False), init)
    _, l, acc = block(qi, carry, True)
    o_ref[0] = (acc / l).astype(BF16)


def _attention(q, kn, kr, v):
    B, S, _ = q.shape
    tq = TQ_ATTN
    return pl.pallas_call(
        functools.partial(_attn_kernel, tq=tq),
        grid=(B, MLA_HEADS, S // tq),
        in_specs=[pl.BlockSpec((1, tq, HEAD_PAD), lambda b, h, i: (b, i, h)),
                  pl.BlockSpec((1, S, LANES), lambda b, h, i: (b, 0, h)),
                  pl.BlockSpec((1, S, LANES), lambda b, h, i: (b, 0, 0)),
                  pl.BlockSpec((1, S, LANES), lambda b, h, i: (b, 0, h))],
        out_specs=pl.BlockSpec((1, tq, MLA_V), lambda b, h, i: (b, i, h)),
        out_shape=jax.ShapeDtypeStruct((B, S, MLA_HEADS * MLA_V), BF16),
        scratch_shapes=[pltpu.VMEM((S, HEAD_PAD), BF16)],
        compiler_params=_cparams("parallel", "parallel", "arbitrary"),
        name="mla_attention",
    )(q, kn, kr, v)


def _tail_kernel(x_ref, m_ref, wo_ref, bo_ref, g_ref, win_ref, wout_ref, o_ref):
    x1 = x_ref[...] + _dot(m_ref[...], wo_ref[...]) + bo_ref[...]
    h = _rms(x1, g_ref[...]).astype(BF16)
    acc = x1
    for kk in range(D_FF // TF_TAIL):
        a = jnp.maximum(_dot(h, win_ref[:, kk * TF_TAIL:(kk + 1) * TF_TAIL]), 0.0)
        acc = acc + _dot((a * a).astype(BF16), wout_ref[kk * TF_TAIL:(kk + 1) * TF_TAIL, :])
    o_ref[...] = acc


def _tail(x2, m2, wo, bo, g, win, wout):
    T = x2.shape[0]
    tm = TM_TAIL
    return pl.pallas_call(
        _tail_kernel,
        grid=(T // tm,),
        in_specs=[_row_spec(tm, D_MODEL), _row_spec(tm, D_MODEL), _resident_spec(wo.shape),
                  _const_spec((1, D_MODEL)), _const_spec((1, D_MODEL)),
                  _resident_spec(win.shape), _resident_spec(wout.shape)],
        out_specs=_row_spec(tm, D_MODEL),
        out_shape=jax.ShapeDtypeStruct((T, D_MODEL), F32),
        compiler_params=_cparams("parallel"),
        name="tail_mlp",
    )(x2, m2, wo, bo, g, win, wout)


def _hg_lb_kernel(logits_ref, lb_ref):
    z = logits_ref[...]
    e = jnp.exp(z - jnp.max(z, axis=0, keepdims=True))
    sm = e / jnp.sum(e, axis=0, keepdims=True)
    depth = z.shape[0]
    acc = sm[0:1]
    first = acc
    lb_ref[0:1, :] = acc - first
    for i in range(1, depth):
        acc = acc + sm[i:i + 1]
        lb_ref[i:i + 1, :] = acc - first


def _hg_lb_table(logits):
    return pl.pallas_call(
        _hg_lb_kernel,
        out_shape=jax.ShapeDtypeStruct(logits.shape, F32),
        name="hgrn_lower_bounds",
    )(logits)


def _hg_proj_kernel(x_ref, g_ref, w_ref, lb_ref, q_ref, k_ref, v_ref, lf_ref, gate_ref):
    h = _rms(x_ref[...], g_ref[...]).astype(BF16)
    lb = lb_ref[...]
    n = HG_HEADS * HG_D
    q_ref[...] = _dot(h, w_ref[:, 0:n]).astype(BF16)
    fz = _dot(h, w_ref[:, n:2 * n])
    lf_ref[...] = jnp.log(lb + (1.0 - lb) * jax.nn.sigmoid(fz))
    k_ref[...] = ((1.0 - lb) * jax.nn.sigmoid(-fz)).astype(BF16)
    v_ref[...] = _dot(h, w_ref[:, 2 * n:3 * n]).astype(BF16)
    gz = _dot(h, w_ref[:, 3 * n:4 * n])
    gate_ref[...] = (gz * jax.nn.sigmoid(gz)).astype(BF16)


def _hg_proj(x2, g, w, lb):
    T = x2.shape[0]
    tm = TM_PROJ
    n = HG_HEADS * HG_D
    bf = jax.ShapeDtypeStruct((T, n), BF16)
    return pl.pallas_call(
        _hg_proj_kernel,
        grid=(T // tm,),
        in_specs=[_row_spec(tm, D_MODEL), _const_spec((1, D_MODEL)), _resident_spec(w.shape),
                  _const_spec((1, n))],
        out_specs=[_row_spec(tm, n)] * 5,
        out_shape=[bf, bf, bf, jax.ShapeDtypeStruct((T, n), F32), bf],
        compiler_params=_cparams("parallel"),
        name="hgrn_proj",
    )(x2, g, w, lb)


HG_LEVELS = 6
HG_SEGS = 2 + HG_LEVELS


def _hg_constants():
    C = HG_CHUNK
    t = np.arange(C)
    j = t[None, :]
    mats = [j <= t[:, None],
            j > t[:, None]]
    masks = []
    for lvl in range(HG_LEVELS):
        half = (C // 2) >> lvl
        blk = t // (2 * half)
        second = (t % (2 * half)) >= half
        mid = blk * 2 * half + half - 1
        a_q = (j > mid[:, None]) & (j <= t[:, None])
        a_k = (j > t[:, None]) & (j <= mid[:, None])
        mats.append(np.where(second[:, None], a_q, a_k))
        masks.append((blk[:, None] == blk[None, :]) & second[:, None] & (~second)[None, :])
    masks.append(np.eye(C, dtype=bool))
    a = np.concatenate(mats, axis=0).astype(np.float32)
    a3 = np.concatenate([a, a, a], axis=1)
    return jnp.asarray(a3, BF16), jnp.asarray(np.stack(masks).astype(np.float32))


def _hg_scan_kernel(q_ref, k_ref, v_ref, lf_ref, gate_ref, a3_ref, mask_ref, gout_ref, o_ref, state_ref):
    C = HG_CHUNK
    n_chunks = q_ref.shape[1] // C
    state_ref[...] = jnp.zeros_like(state_ref)
    gout = gout_ref[...]

    def chunk(c, carry):
        rows = pl.ds(pl.multiple_of(c * C, C), C)
        lf = lf_ref[0, rows, :]
        hi = lf.astype(BF16)
        r1 = lf - hi.astype(F32)
        mid = r1.astype(BF16)
        lo = (r1 - mid.astype(F32)).astype(BF16)
        x = jnp.exp(_dot(a3_ref[...], jnp.concatenate([hi, mid, lo], axis=0)))
        q = q_ref[0, rows, :].astype(F32)
        k = k_ref[0, rows, :].astype(F32)
        v = v_ref[0, rows, :]
        attn = _dot_nt(q.astype(BF16), k.astype(BF16)) * mask_ref[HG_LEVELS]
        for lvl in range(HG_LEVELS):
            xl = x[(2 + lvl) * C:(3 + lvl) * C]
            attn = attn + _dot_nt((q * xl).astype(BF16), (k * xl).astype(BF16)) * mask_ref[lvl]
        state = state_ref[...]
        o = _dot_nt((q * x[0:C]).astype(BF16), state.astype(BF16)) + _dot(attn.astype(BF16), v)
        k_state = (k * x[C:2 * C]).astype(BF16)
        upd = lax.dot_general(v, k_state, (((0,), (0,)), ((), ())), preferred_element_type=F32)
        state_ref[...] = state * x[C - 1:C] + upd
        o = o * lax.rsqrt(jnp.mean(o * o, axis=-1, keepdims=True) + EPS) * gout
        o_ref[0, rows, :] = (o * gate_ref[0, rows, :].astype(F32)).astype(BF16)
        return carry

    lax.fori_loop(0, n_chunks, chunk, 0)


def _hg_scan(q, k, v, lf, gate, gout):
    B, S, _ = q.shape
    a3, masks = _hg_constants()
    blk = pl.BlockSpec((1, S, HG_D), lambda b, h: (b, 0, h))
    return pl.pallas_call(
        _hg_scan_kernel,
        grid=(B, HG_HEADS),
        in_specs=[blk, blk, blk, blk, blk, _const_spec(a3.shape), _const_spec(masks.shape),
                  _const_spec((1, HG_D))],
        out_specs=blk,
        out_shape=jax.ShapeDtypeStruct((B, S, HG_HEADS * HG_D), BF16),
        scratch_shapes=[pltpu.VMEM((HG_D, HG_D), F32)],
        compiler_params=_cparams("parallel", "parallel"),
        name="hgrn_scan",
    )(q, k, v, lf, gate, a3, masks, gout)


def _cv_glu_kernel(x_ref, g_ref, w_ref, b_ref, u_ref):
    h = _rms(x_ref[...], g_ref[...]).astype(BF16)
    a = _dot(h, w_ref[:, :D_MODEL]) + b_ref[:, :D_MODEL]
    gate = _dot(h, w_ref[:, D_MODEL:]) + b_ref[:, D_MODEL:]
    u_ref[...] = a * jax.nn.sigmoid(gate)


def _cv_glu(x2, g, w, b):
    T = x2.shape[0]
    tm = TM_PROJ
    return pl.pallas_call(
        _cv_glu_kernel,
        grid=(T // tm,),
        in_specs=[_row_spec(tm, D_MODEL), _const_spec((1, D_MODEL)), _resident_spec(w.shape),
                  _const_spec((1, 2 * D_MODEL))],
        out_specs=_row_spec(tm, D_MODEL),
        out_shape=jax.ShapeDtypeStruct((T, D_MODEL), F32),
        compiler_params=_cparams("parallel"),
        name="conv_glu",
    )(x2, g, w, b)


def _cv_dw_kernel(u_ref, prev_ref, w_ref, b_ref, lng_ref, lnb_ref, o_ref, buf_ref, *, tm):
    i = pl.program_id(1)
    halo = prev_ref[0]
    buf_ref[0:CONV_HALO, :] = jnp.where(i == 0, 0.0, halo)
    buf_ref[CONV_HALO:, :] = u_ref[0]
    off = CONV_HALO - (CONV_WIDTH - 1)
    acc = jnp.zeros((tm, D_MODEL), F32) + b_ref[...]
    for j in range(CONV_WIDTH):
        acc = acc + buf_ref[off + j:off + j + tm, :] * w_ref[j:j + 1, :]
    mu = jnp.mean(acc, axis=-1, keepdims=True)
    xc = acc - mu
    y = xc * lax.rsqrt(jnp.mean(xc * xc, axis=-1, keepdims=True) + EPS) * lng_ref[...] + lnb_ref[...]
    o_ref[0] = (y * jax.nn.sigmoid(y)).astype(BF16)


def _cv_dw(u, w, b, lng, lnb):
    B, S, _ = u.shape
    tm = TM_CONV
    ratio = tm // CONV_HALO
    return pl.pallas_call(
        functools.partial(_cv_dw_kernel, tm=tm),
        grid=(B, S // tm),
        in_specs=[pl.BlockSpec((1, tm, D_MODEL), lambda b, i: (b, i, 0)),
                  pl.BlockSpec((1, CONV_HALO, D_MODEL), lambda b, i: (b, jnp.maximum(i * ratio - 1, 0), 0)),
                  _const_spec((CONV_WIDTH, D_MODEL)), _const_spec((1, D_MODEL)),
                  _const_spec((1, D_MODEL)), _const_spec((1, D_MODEL))],
        out_specs=pl.BlockSpec((1, tm, D_MODEL), lambda b, i: (b, i, 0)),
        out_shape=jax.ShapeDtypeStruct((B, S, D_MODEL), BF16),
        scratch_shapes=[pltpu.VMEM((tm + CONV_HALO, D_MODEL), F32)],
        compiler_params=_cparams("parallel", "parallel"),
        name="conv_depthwise",
    )(u, u, w, b, lng, lnb)


def _dup_rope(t):
    return jnp.concatenate([t, t], axis=-1)


def _mla_weights(w_down, w_uq, q_head_norm, k_head_norm):
    wd = jnp.concatenate([w_down, w_down[:, -MLA_ROPE:]], axis=1).astype(BF16)
    wq = w_uq.reshape(MLA_Q_LORA, MLA_HEADS, MLA_QK)
    wq = jnp.concatenate([wq, wq[:, :, MLA_NOPE:]], axis=-1)
    wq = wq.reshape(MLA_Q_LORA, MLA_HEADS * HEAD_PAD).astype(BF16)
    qng = q_head_norm[None, :MLA_NOPE]
    qrg = _dup_rope(q_head_norm[None, MLA_NOPE:])
    kng = k_head_norm[None, :MLA_NOPE]
    krg = _dup_rope(k_head_norm[None, MLA_NOPE:])
    return wd, wq, qng, qrg, kng, krg


def kernel(x, positions, norm_mix, norm_mlp, mlp_w_in, mlp_w_out, mla_w_down, mla_q_lat_norm, mla_kv_lat_norm, mla_w_uq, mla_w_ukv, mla_q_head_norm, mla_k_head_norm, mla_w_o, hg_w_in, hg_lb_logits, hg_out_norm, hg_w_o, cv_w_pw1, cv_b_pw1, cv_w_dw, cv_b_dw, cv_ln_g, cv_ln_b, cv_w_pw2, cv_b_pw2):
    B, S, D = x.shape
    T = B * S
    depth = norm_mix.shape[0]
    x2 = x.reshape(T, D)

    inv_freq = jnp.power(ROPE_THETA, -jnp.arange(0, MLA_ROPE, 2, dtype=F32) / MLA_ROPE)
    invf4 = jnp.tile(inv_freq, 4)[None, :]
    cos, sin = _rope_tables(positions.astype(F32).reshape(T, 1), invf4)
    lb_table = _hg_lb_table(hg_lb_logits.astype(F32))
    zero_bias = jnp.zeros((1, D), F32)

    for layer in range(depth):
        kind, j = layer % N_MIXERS, layer // N_MIXERS
        g_mix = norm_mix[layer][None, :]
        if kind == 0:
            wd, wq, qng, qrg, kng, krg = _mla_weights(mla_w_down[j], mla_w_uq[j], mla_q_head_norm[j],
                                                      mla_k_head_norm[j])
            q, kn, kr, v = _mla_proj(x2, g_mix, wd, mla_q_lat_norm[j][None, :], mla_kv_lat_norm[j][None, :],
                                     wq, mla_w_ukv[j].astype(BF16), qng, qrg, kng, krg, cos, sin)
            m = _attention(q.reshape(B, S, -1), kn.reshape(B, S, -1), kr.reshape(B, S, -1),
                           v.reshape(B, S, -1)).reshape(T, D)
            w_o, b_o = mla_w_o[j], zero_bias
        elif kind == 1:
            q, k, v, lf, gate = _hg_proj(x2, g_mix, hg_w_in[j].astype(BF16), lb_table[layer][None, :])
            r3 = lambda t: t.reshape(B, S, -1)
            m = _hg_scan(r3(q), r3(k), r3(v), r3(lf), r3(gate), hg_out_norm[j][None, :]).reshape(T, D)
            w_o, b_o = hg_w_o[j], zero_bias
        else:
            u = _cv_glu(x2, g_mix, cv_w_pw1[j].astype(BF16), cv_b_pw1[j][None, :])
            m = _cv_dw(u.reshape(B, S, D), cv_w_dw[j], cv_b_dw[j][None, :], cv_ln_g[j][None, :],
                       cv_ln_b[j][None, :]).reshape(T, D)
            w_o, b_o = cv_w_pw2[j], cv_b_pw2[j][None, :]
        x2 = _tail(x2, m, w_o.astype(BF16), b_o, norm_mlp[layer][None, :],
                   mlp_w_in[layer].astype(BF16), mlp_w_out[layer].astype(BF16))
    return x2.reshape(B, S, D)
```

```python
import functools

import numpy as np
import jax
import jax.numpy as jnp
from jax import lax
from jax.experimental import pallas as pl
from jax.experimental.pallas import tpu as pltpu

F32 = jnp.float32
BF16 = jnp.bfloat16

D_MODEL = 1024
N_MIXERS = 3
MLA_HEADS = 8
MLA_Q_LORA = 384
MLA_KV_LORA = 256
MLA_NOPE = 128
MLA_ROPE = 64
MLA_V = 128
MLA_QK = MLA_NOPE + MLA_ROPE
ROPE_THETA = 10000.0
HG_HEADS = 8
HG_D = D_MODEL // HG_HEADS
HG_CHUNK = 64
CONV_WIDTH = 31
D_FF = 4 * D_MODEL
EPS = 1e-6
LOG2_E = 1.4426950408889634

LANES = 128
HEAD_PAD = 2 * LANES
VMEM_LIMIT = 56 * 1024 * 1024

TM_PROJ = 512
TM_TAIL = 512
TF_TAIL = 512
TQ_ATTN = 512
TM_CONV = 512
CONV_HALO = 32
CONV_ROWS = 32
SUBLANES = 8


def _cparams(*sem):
    return pltpu.CompilerParams(dimension_semantics=sem, vmem_limit_bytes=VMEM_LIMIT)


def _rms(x, g):
    return x * lax.rsqrt(jnp.mean(x * x, axis=-1, keepdims=True) + EPS) * g


def _dot(a, b):
    return jnp.dot(a, b, preferred_element_type=F32)


def _dot_nt(a, b):
    return lax.dot_general(a, b, (((1,), (1,)), ((), ())), preferred_element_type=F32)


def _row_spec(tm, n):
    return pl.BlockSpec((tm, n), lambda i: (i, 0))


def _const_spec(shape):
    return pl.BlockSpec(shape, lambda *_: (0,) * len(shape))


def _resident_spec(shape):
    return pl.BlockSpec(shape, lambda *_: (0,) * len(shape), pipeline_mode=pl.Buffered(1))


def _rope_table_kernel(pos_ref, invf_ref, cos_ref, sin_ref):
    ang = pos_ref[...] * invf_ref[...]
    lane = lax.broadcasted_iota(jnp.int32, (1, LANES), 1)
    sign = jnp.where((lane // (MLA_ROPE // 2)) % 2 == 0, -1.0, 1.0)
    cos_ref[...] = jnp.cos(ang)
    sin_ref[...] = jnp.sin(ang) * sign


def _rope_tables(pos_col, invf4):
    T = pos_col.shape[0]
    tm = TM_PROJ
    return pl.pallas_call(
        _rope_table_kernel,
        grid=(T // tm,),
        in_specs=[_row_spec(tm, 1), _const_spec((1, LANES))],
        out_specs=[_row_spec(tm, LANES), _row_spec(tm, LANES)],
        out_shape=[jax.ShapeDtypeStruct((T, LANES), F32)] * 2,
        compiler_params=_cparams("parallel"),
        name="rope_tables",
    )(pos_col, invf4)


def _mla_proj_kernel(x_ref, g_ref, wd_ref, qlg_ref, kvlg_ref, wuq_ref, wukv_ref,
                     qng_ref, qrg_ref, kng_ref, krg_ref, cos_ref, sin_ref,
                     q_ref, kn_ref, kr_ref, v_ref):
    h = _rms(x_ref[...], g_ref[...]).astype(BF16)
    lat = _dot(h, wd_ref[...])
    c_q = _rms(lat[:, :MLA_Q_LORA], qlg_ref[...]).astype(BF16)
    c_kv = _rms(lat[:, MLA_Q_LORA:MLA_Q_LORA + MLA_KV_LORA], kvlg_ref[...]).astype(BF16)
    cos = cos_ref[...]
    sin = sin_ref[...]

    def rope(r):
        return r * cos + pltpu.roll(r, MLA_ROPE // 2, 1) * sin

    lane = lax.broadcasted_iota(jnp.int32, (1, LANES), 1)
    kr = rope(_rms(lat[:, MLA_Q_LORA + MLA_KV_LORA:], krg_ref[...]))
    kr_ref[...] = jnp.where(lane < MLA_ROPE, kr, 0.0).astype(BF16)

    scale = MLA_QK ** -0.5 * LOG2_E
    qng = qng_ref[...] * scale
    qrg = qrg_ref[...] * scale
    kng = kng_ref[...]
    for hh in range(MLA_HEADS):
        q = _dot(c_q, wuq_ref[:, hh * HEAD_PAD:(hh + 1) * HEAD_PAD])
        q_ref[:, hh * HEAD_PAD:hh * HEAD_PAD + LANES] = _rms(q[:, :LANES], qng).astype(BF16)
        q_ref[:, hh * HEAD_PAD + LANES:(hh + 1) * HEAD_PAD] = rope(_rms(q[:, LANES:], qrg)).astype(BF16)
        kv = _dot(c_kv, wukv_ref[:, hh * 2 * LANES:(hh + 1) * 2 * LANES])
        kn_ref[:, hh * LANES:(hh + 1) * LANES] = _rms(kv[:, :LANES], kng).astype(BF16)
        v_ref[:, hh * LANES:(hh + 1) * LANES] = kv[:, LANES:].astype(BF16)


def _mla_proj(x2, g, wd, qlg, kvlg, wuq, wukv, qng, qrg, kng, krg, cos, sin):
    T = x2.shape[0]
    tm = TM_PROJ
    n_lat = wd.shape[1]
    return pl.pallas_call(
        _mla_proj_kernel,
        grid=(T // tm,),
        in_specs=[_row_spec(tm, D_MODEL), _const_spec((1, D_MODEL)), _resident_spec(wd.shape),
                  _const_spec((1, MLA_Q_LORA)), _const_spec((1, MLA_KV_LORA)),
                  _resident_spec(wuq.shape), _resident_spec(wukv.shape),
                  _const_spec((1, LANES)), _const_spec((1, LANES)), _const_spec((1, LANES)),
                  _const_spec((1, LANES)), _row_spec(tm, LANES), _row_spec(tm, LANES)],
        out_specs=[_row_spec(tm, MLA_HEADS * HEAD_PAD), _row_spec(tm, MLA_HEADS * LANES),
                   _row_spec(tm, LANES), _row_spec(tm, MLA_HEADS * LANES)],
        out_shape=[jax.ShapeDtypeStruct((T, MLA_HEADS * HEAD_PAD), BF16),
                   jax.ShapeDtypeStruct((T, MLA_HEADS * LANES), BF16),
                   jax.ShapeDtypeStruct((T, LANES), BF16),
                   jax.ShapeDtypeStruct((T, MLA_HEADS * LANES), BF16)],
        compiler_params=_cparams("parallel"),
        name="mla_proj",
    )(x2, g, wd, qlg, kvlg, wuq, wukv, qng, qrg, kng, krg, cos, sin)


def _attn_kernel(q_ref, kn_ref, kr_ref, v_ref, o_ref, kcat_ref, vt_ref, *, tq):
    qi = pl.program_id(2)

    @pl.when(qi == 0)
    def _():
        kcat_ref[:, :LANES] = kn_ref[0]
        kcat_ref[:, LANES:] = kr_ref[0]
        vt_ref[...] = v_ref[0].astype(F32).T.astype(BF16)

    q = q_ref[0]

    def scores(j):
        start = pl.multiple_of(j * tq, tq)
        return _dot_nt(kcat_ref[pl.ds(start, tq), :], q)

    def accumulate(j, s, carry, masked):
        m, l, acc = carry
        if masked:
            key = lax.broadcasted_iota(jnp.int32, (tq, tq), 0)
            qry = lax.broadcasted_iota(jnp.int32, (tq, tq), 1)
            s = jnp.where(key <= qry, s, -jnp.inf)
        m_new = jnp.maximum(m, jnp.max(s, axis=0, keepdims=True))
        alpha = jnp.exp2(m - m_new)
        p = jnp.exp2(s - m_new)
        l = alpha * l + jnp.sum(p, axis=0, keepdims=True)
        start = pl.multiple_of(j * tq, tq)
        acc = alpha * acc + _dot(vt_ref[:, pl.ds(start, tq)], p.astype(BF16))
        return m_new, l, acc

    def step(j, carry):
        s_next = scores(j + 1)
        return (s_next,) + accumulate(j, carry[0], carry[1:], False)

    init = (scores(0), jnp.full((1, tq), -jnp.inf, F32), jnp.zeros((1, tq), F32), jnp.zeros((MLA_V, tq), F32))
    carry = lax.fori_loop(0, qi, step, init)
    _, l, acc = accumulate(qi, carry[0], carry[1:], True)
    o_ref[0] = (acc / l).T.astype(BF16)


def _attention(q, kn, kr, v):
    B, S, _ = q.shape
    tq = TQ_ATTN
    return pl.pallas_call(
        functools.partial(_attn_kernel, tq=tq),
        grid=(B, MLA_HEADS, S // tq),
        in_specs=[pl.BlockSpec((1, tq, HEAD_PAD), lambda b, h, i: (b, i, h)),
                  pl.BlockSpec((1, S, LANES), lambda b, h, i: (b, 0, h)),
                  pl.BlockSpec((1, S, LANES), lambda b, h, i: (b, 0, 0)),
                  pl.BlockSpec((1, S, LANES), lambda b, h, i: (b, 0, h))],
        out_specs=pl.BlockSpec((1, tq, MLA_V), lambda b, h, i: (b, i, h)),
        out_shape=jax.ShapeDtypeStruct((B, S, MLA_HEADS * MLA_V), BF16),
        scratch_shapes=[pltpu.VMEM((S, HEAD_PAD), BF16), pltpu.VMEM((MLA_V, S), BF16)],
        compiler_params=_cparams("parallel", "parallel", "arbitrary"),
        name="mla_attention",
    )(q, kn, kr, v)


def _tail_kernel(x_ref, m_ref, wo_ref, bo_ref, g_ref, win_ref, wout_ref, o_ref):
    x1 = x_ref[...] + _dot(m_ref[...], wo_ref[...]) + bo_ref[...]
    h = _rms(x1, g_ref[...]).astype(BF16)
    acc = x1
    for kk in range(D_FF // TF_TAIL):
        a = jnp.maximum(_dot(h, win_ref[:, kk * TF_TAIL:(kk + 1) * TF_TAIL]), 0.0)
        acc = acc + _dot((a * a).astype(BF16), wout_ref[kk * TF_TAIL:(kk + 1) * TF_TAIL, :])
    o_ref[...] = acc


def _tail(x2, m2, wo, bo, g, win, wout):
    T = x2.shape[0]
    tm = TM_TAIL
    return pl.pallas_call(
        _tail_kernel,
        grid=(T // tm,),
        in_specs=[_row_spec(tm, D_MODEL), _row_spec(tm, D_MODEL), _resident_spec(wo.shape),
                  _const_spec((1, D_MODEL)), _const_spec((1, D_MODEL)),
                  _resident_spec(win.shape), _resident_spec(wout.shape)],
        out_specs=_row_spec(tm, D_MODEL),
        out_shape=jax.ShapeDtypeStruct((T, D_MODEL), F32),
        compiler_params=_cparams("parallel"),
        name="tail_mlp",
    )(x2, m2, wo, bo, g, win, wout)


def _hg_lb_kernel(logits_ref, lb_ref):
    z = logits_ref[...]
    e = jnp.exp(z - jnp.max(z, axis=0, keepdims=True))
    sm = e / jnp.sum(e, axis=0, keepdims=True)
    depth = z.shape[0]
    acc = sm[0:1]
    first = acc
    lb_ref[0:1, :] = acc - first
    for i in range(1, depth):
        acc = acc + sm[i:i + 1]
        lb_ref[i:i + 1, :] = acc - first


def _hg_lb_table(logits):
    return pl.pallas_call(
        _hg_lb_kernel,
        out_shape=jax.ShapeDtypeStruct(logits.shape, F32),
        name="hgrn_lower_bounds",
    )(logits)


def _hg_proj_kernel(x_ref, g_ref, w_ref, lb_ref, q_ref, k_ref, v_ref, lf_ref, gate_ref):
    h = _rms(x_ref[...], g_ref[...]).astype(BF16)
    lb = lb_ref[...]
    n = HG_HEADS * HG_D
    q_ref[...] = _dot(h, w_ref[:, 0:n]).astype(BF16)
    fz = _dot(h, w_ref[:, n:2 * n])
    lf_ref[...] = jnp.log(lb + (1.0 - lb) * jax.nn.sigmoid(fz))
    k_ref[...] = ((1.0 - lb) * jax.nn.sigmoid(-fz)).astype(BF16)
    v_ref[...] = _dot(h, w_ref[:, 2 * n:3 * n]).astype(BF16)
    gz = _dot(h, w_ref[:, 3 * n:4 * n])
    gate_ref[...] = (gz * jax.nn.sigmoid(gz)).astype(BF16)


def _hg_proj(x2, g, w, lb):
    T = x2.shape[0]
    tm = TM_PROJ
    n = HG_HEADS * HG_D
    bf = jax.ShapeDtypeStruct((T, n), BF16)
    return pl.pallas_call(
        _hg_proj_kernel,
        grid=(T // tm,),
        in_specs=[_row_spec(tm, D_MODEL), _const_spec((1, D_MODEL)), _resident_spec(w.shape),
                  _const_spec((1, n))],
        out_specs=[_row_spec(tm, n)] * 5,
        out_shape=[bf, bf, bf, jax.ShapeDtypeStruct((T, n), F32), bf],
        compiler_params=_cparams("parallel"),
        name="hgrn_proj",
    )(x2, g, w, lb)


HG_LEVELS = 6
HG_COARSE = 3
HG_GROUP = 8
HG_OUT_GROUP = 8


def _hg_level(lvl):
    C = HG_CHUNK
    t = np.arange(C)
    half = (C // 2) >> lvl
    blk = t // (2 * half)
    second = (t % (2 * half)) >= half
    mid = blk * 2 * half + half - 1
    return half, blk, second, mid


def _hg_constants():
    C = HG_CHUNK
    t = np.arange(C)
    j = t[None, :]
    mats = [j <= t[:, None]]
    signs, masks = [], []
    for lvl in range(HG_LEVELS):
        half, blk, second, mid = _hg_level(lvl)
        if lvl < HG_COARSE:
            signs.append(np.where(second, 1.0, -1.0)[:, None] * np.ones((1, HG_D)))
        else:
            a_q = (j > mid[:, None]) & (j <= t[:, None])
            a_k = (j > t[:, None]) & (j <= mid[:, None])
            mats.append(np.where(second[:, None], a_q, a_k))
        masks.append((blk[:, None] == blk[None, :]) & second[:, None] & (~second)[None, :])
    masks.append(np.eye(C, dtype=bool))
    a = np.concatenate(mats, axis=0).astype(np.float32)
    a3 = np.concatenate([a, a, a], axis=1)
    return (jnp.asarray(a3, BF16), jnp.asarray(np.stack(signs), F32),
            jnp.asarray(np.stack(masks).astype(np.float32)))


def _hg_scan_kernel(q_ref, k_ref, v_ref, lf_ref, gate_ref, a3_ref, sign_ref, mask_ref, gout_ref, o_ref,
                    qh_ref, oi_ref, st_ref):
    C = HG_CHUNK
    n_chunks = q_ref.shape[1] // C
    gout = gout_ref[...]

    G = HG_GROUP
    grp = range(G)

    def group(gi, state):
        rows = [pl.ds(pl.multiple_of((gi * G + c) * C, C), C) for c in grp]
        lf3 = []
        for c in grp:
            lf = lf_ref[0, rows[c], :]
            hi = lf.astype(BF16)
            r1 = lf - hi.astype(F32)
            mid = r1.astype(BF16)
            lo = (r1 - mid.astype(F32)).astype(BF16)
            lf3.append(jnp.concatenate([hi, mid, lo], axis=0))
        seg = _dot(a3_ref[...], jnp.concatenate(lf3, axis=1))
        seg = [seg[:, c * HG_D:(c + 1) * HG_D] for c in grp]
        b = [seg[c][0:C] for c in grp]
        q = [q_ref[0, rows[c], :].astype(F32) for c in grp]
        k = [k_ref[0, rows[c], :].astype(F32) for c in grp]
        v = [v_ref[0, rows[c], :] for c in grp]
        attn = [_dot_nt(q[c].astype(BF16), k[c].astype(BF16)) * mask_ref[HG_LEVELS] for c in grp]
        upd = [lax.dot_general(v[c], (k[c] * jnp.exp(b[c][C - 1:C] - b[c])).astype(BF16),
                               (((0,), (0,)), ((), ())), preferred_element_type=F32) for c in grp]
        for c in grp:
            qh_ref[rows[c], :] = (q[c] * jnp.exp(b[c])).astype(BF16)
        for lvl in range(HG_LEVELS):
            for c in grp:
                if lvl < HG_COARSE:
                    half, _, _, mids = _hg_level(lvl)
                    b_mid = jnp.concatenate([jnp.broadcast_to(b[c][m:m + 1], (2 * half, HG_D))
                                             for m in mids[::2 * half]], axis=0)
                    e = (b[c] - b_mid) * sign_ref[lvl]
                else:
                    e = seg[c][(1 + lvl - HG_COARSE) * C:(2 + lvl - HG_COARSE) * C]
                xl = jnp.exp(e)
                attn[c] = attn[c] + _dot_nt((q[c] * xl).astype(BF16), (k[c] * xl).astype(BF16)) * mask_ref[lvl]
        for c in grp:
            oi_ref[rows[c], :] = _dot(attn[c].astype(BF16), v[c])
            st_ref[gi * G + c] = state.astype(BF16)
            state = state * jnp.exp(b[c][C - 1:C]) + upd[c]
        return state

    lax.fori_loop(0, n_chunks // G, group, jnp.zeros((HG_D, HG_D), F32))

    GO = HG_OUT_GROUP
    out_grp = range(GO)

    def out_group(gi, carry):
        rows = [pl.ds(pl.multiple_of((gi * GO + c) * C, C), C) for c in out_grp]
        o = [_dot_nt(qh_ref[rows[c], :], st_ref[gi * GO + c]) + oi_ref[rows[c], :] for c in out_grp]
        o = [o[c] * lax.rsqrt(jnp.mean(o[c] * o[c], axis=-1, keepdims=True) + EPS) * gout for c in out_grp]
        for c in out_grp:
            o_ref[0, rows[c], :] = (o[c] * gate_ref[0, rows[c], :].astype(F32)).astype(BF16)
        return carry

    lax.fori_loop(0, n_chunks // GO, out_group, 0)


def _hg_scan(q, k, v, lf, gate, gout):
    B, S, _ = q.shape
    a3, signs, masks = _hg_constants()
    blk = pl.BlockSpec((1, S, HG_D), lambda b, h: (b, 0, h))
    return pl.pallas_call(
        _hg_scan_kernel,
        grid=(B, HG_HEADS),
        in_specs=[blk, blk, blk, blk, blk, _const_spec(a3.shape), _const_spec(signs.shape),
                  _const_spec(masks.shape), _const_spec((1, HG_D))],
        out_specs=blk,
        out_shape=jax.ShapeDtypeStruct((B, S, HG_HEADS * HG_D), BF16),
        scratch_shapes=[pltpu.VMEM((S, HG_D), BF16),
                        pltpu.VMEM((S, HG_D), F32),
                        pltpu.VMEM((S // HG_CHUNK, HG_D, HG_D), BF16)],
        compiler_params=_cparams("parallel", "parallel"),
        name="hgrn_scan",
    )(q, k, v, lf, gate, a3, signs, masks, gout)


def _cv_glu_kernel(x_ref, g_ref, w_ref, b_ref, u_ref):
    h = _rms(x_ref[...], g_ref[...]).astype(BF16)
    a = _dot(h, w_ref[:, :D_MODEL]) + b_ref[:, :D_MODEL]
    gate = _dot(h, w_ref[:, D_MODEL:]) + b_ref[:, D_MODEL:]
    u_ref[...] = a * jax.nn.sigmoid(gate)


def _cv_glu(x2, g, w, b):
    T = x2.shape[0]
    tm = TM_PROJ
    return pl.pallas_call(
        _cv_glu_kernel,
        grid=(T // tm,),
        in_specs=[_row_spec(tm, D_MODEL), _const_spec((1, D_MODEL)), _resident_spec(w.shape),
                  _const_spec((1, 2 * D_MODEL))],
        out_specs=_row_spec(tm, D_MODEL),
        out_shape=jax.ShapeDtypeStruct((T, D_MODEL), F32),
        compiler_params=_cparams("parallel"),
        name="conv_glu",
    )(x2, g, w, b)


def _cv_dw_kernel(u_ref, prev_ref, w_ref, b_ref, lng_ref, lnb_ref, o_ref, buf_ref, sh_ref, acc_ref, *, tm):
    i = pl.program_id(1)
    halo = prev_ref[0]
    buf_ref[0:CONV_HALO, :] = jnp.where(i == 0, 0.0, halo)
    buf_ref[CONV_HALO:, :] = u_ref[0]
    n_sh = sh_ref.shape[1]
    for r in range(1, SUBLANES):
        sh_ref[r - 1] = buf_ref[r:r + n_sh, :]
    off = CONV_HALO - (CONV_WIDTH - 1)
    rb = CONV_ROWS
    for ct in range(D_MODEL // LANES):
        cols = slice(ct * LANES, (ct + 1) * LANES)
        taps = [w_ref[j:j + 1, cols] for j in range(CONV_WIDTH)]
        for base in range(0, tm, rb):
            acc = jnp.zeros((rb, LANES), F32) + b_ref[:, cols]
            for j in range(CONV_WIDTH):
                a, r = divmod(off + j, SUBLANES)
                lo = base + a * SUBLANES
                src = buf_ref[lo:lo + rb, cols] if r == 0 else sh_ref[r - 1, lo:lo + rb, cols]
                acc = acc + src * taps[j]
            acc_ref[base:base + rb, cols] = acc
    acc = acc_ref[...]
    mu = jnp.mean(acc, axis=-1, keepdims=True)
    xc = acc - mu
    y = xc * lax.rsqrt(jnp.mean(xc * xc, axis=-1, keepdims=True) + EPS) * lng_ref[...] + lnb_ref[...]
    o_ref[0] = (y * jax.nn.sigmoid(y)).astype(BF16)


def _cv_dw(u, w, b, lng, lnb):
    B, S, _ = u.shape
    tm = TM_CONV
    ratio = tm // CONV_HALO
    return pl.pallas_call(
        functools.partial(_cv_dw_kernel, tm=tm),
        grid=(B, S // tm),
        in_specs=[pl.BlockSpec((1, tm, D_MODEL), lambda b, i: (b, i, 0)),
                  pl.BlockSpec((1, CONV_HALO, D_MODEL), lambda b, i: (b, jnp.maximum(i * ratio - 1, 0), 0)),
                  _const_spec((CONV_WIDTH, D_MODEL)), _const_spec((1, D_MODEL)),
                  _const_spec((1, D_MODEL)), _const_spec((1, D_MODEL))],
        out_specs=pl.BlockSpec((1, tm, D_MODEL), lambda b, i: (b, i, 0)),
        out_shape=jax.ShapeDtypeStruct((B, S, D_MODEL), BF16),
        scratch_shapes=[pltpu.VMEM((tm + CONV_HALO, D_MODEL), F32),
                        pltpu.VMEM((SUBLANES - 1, tm + CONV_HALO - SUBLANES, D_MODEL), F32),
                        pltpu.VMEM((tm, D_MODEL), F32)],
        compiler_params=_cparams("parallel", "parallel"),
        name="conv_depthwise",
    )(u, u, w, b, lng, lnb)


def _dup_rope(t):
    return jnp.concatenate([t, t], axis=-1)


def _mla_weights(w_down, w_uq, q_head_norm, k_head_norm):
    wd = jnp.concatenate([w_down, w_down[:, -MLA_ROPE:]], axis=1).astype(BF16)
    wq = w_uq.reshape(MLA_Q_LORA, MLA_HEADS, MLA_QK)
    wq = jnp.concatenate([wq, wq[:, :, MLA_NOPE:]], axis=-1)
    wq = wq.reshape(MLA_Q_LORA, MLA_HEADS * HEAD_PAD).astype(BF16)
    qng = q_head_norm[None, :MLA_NOPE]
    qrg = _dup_rope(q_head_norm[None, MLA_NOPE:])
    kng = k_head_norm[None, :MLA_NOPE]
    krg = _dup_rope(k_head_norm[None, MLA_NOPE:])
    return wd, wq, qng, qrg, kng, krg


def kernel(x, positions, norm_mix, norm_mlp, mlp_w_in, mlp_w_out, mla_w_down, mla_q_lat_norm, mla_kv_lat_norm, mla_w_uq, mla_w_ukv, mla_q_head_norm, mla_k_head_norm, mla_w_o, hg_w_in, hg_lb_logits, hg_out_norm, hg_w_o, cv_w_pw1, cv_b_pw1, cv_w_dw, cv_b_dw, cv_ln_g, cv_ln_b, cv_w_pw2, cv_b_pw2):
    B, S, D = x.shape
    T = B * S
    depth = norm_mix.shape[0]
    x2 = x.reshape(T, D)

    inv_freq = jnp.power(ROPE_THETA, -jnp.arange(0, MLA_ROPE, 2, dtype=F32) / MLA_ROPE)
    invf4 = jnp.tile(inv_freq, 4)[None, :]
    cos, sin = _rope_tables(positions.astype(F32).reshape(T, 1), invf4)
    lb_table = _hg_lb_table(hg_lb_logits.astype(F32))
    zero_bias = jnp.zeros((1, D), F32)

    for layer in range(depth):
        kind, j = layer % N_MIXERS, layer // N_MIXERS
        g_mix = norm_mix[layer][None, :]
        if kind == 0:
            wd, wq, qng, qrg, kng, krg = _mla_weights(mla_w_down[j], mla_w_uq[j], mla_q_head_norm[j],
                                                      mla_k_head_norm[j])
            q, kn, kr, v = _mla_proj(x2, g_mix, wd, mla_q_lat_norm[j][None, :], mla_kv_lat_norm[j][None, :],
                                     wq, mla_w_ukv[j].astype(BF16), qng, qrg, kng, krg, cos, sin)
            m = _attention(q.reshape(B, S, -1), kn.reshape(B, S, -1), kr.reshape(B, S, -1),
                           v.reshape(B, S, -1)).reshape(T, D)
            w_o, b_o = mla_w_o[j], zero_bias
        elif kind == 1:
            q, k, v, lf, gate = _hg_proj(x2, g_mix, hg_w_in[j].astype(BF16), lb_table[layer][None, :])
            r3 = lambda t: t.reshape(B, S, -1)
            m = _hg_scan(r3(q), r3(k), r3(v), r3(lf), r3(gate), hg_out_norm[j][None, :]).reshape(T, D)
            w_o, b_o = hg_w_o[j], zero_bias
        else:
            u = _cv_glu(x2, g_mix, cv_w_pw1[j].astype(BF16), cv_b_pw1[j][None, :])
            m = _cv_dw(u.reshape(B, S, D), cv_w_dw[j], cv_b_dw[j][None, :], cv_ln_g[j][None, :],
                       cv_ln_b[j][None, :]).reshape(T, D)
            w_o, b_o = cv_w_pw2[j], cv_b_pw2[j][None, :]
        x2 = _tail(x2, m, w_o.astype(BF16), b_o, norm_mlp[layer][None, :],
                   mlp_w_in[layer].astype(BF16), mlp_w_out[layer].astype(BF16))
    return x2.reshape(B, S, D)
```

```python
import functools

import numpy as np
import jax
import jax.numpy as jnp
from jax import lax
from jax.experimental import pallas as pl
from jax.experimental.pallas import tpu as pltpu

F32 = jnp.float32
BF16 = jnp.bfloat16

D_MODEL = 1024
N_MIXERS = 3
MLA_HEADS = 8
MLA_Q_LORA = 384
MLA_KV_LORA = 256
MLA_NOPE = 128
MLA_ROPE = 64
MLA_V = 128
MLA_QK = MLA_NOPE + MLA_ROPE
ROPE_THETA = 10000.0
HG_HEADS = 8
HG_D = D_MODEL // HG_HEADS
HG_CHUNK = 64
CONV_WIDTH = 31
D_FF = 4 * D_MODEL
EPS = 1e-6
LOG2_E = 1.4426950408889634

LANES = 128
HEAD_PAD = 2 * LANES
VMEM_LIMIT = 56 * 1024 * 1024

TM_PROJ = 512
TM_TAIL = 512
TF_TAIL = 512
TQ_ATTN = 512
TM_CONV = 512
CONV_HALO = 32
CONV_ROWS = 64
CONV_TAP_GROUP = 16
SUBLANES = 8


def _cparams(*sem):
    return pltpu.CompilerParams(dimension_semantics=sem, vmem_limit_bytes=VMEM_LIMIT)


def _rms(x, g):
    return x * lax.rsqrt(jnp.mean(x * x, axis=-1, keepdims=True) + EPS) * g


def _dot(a, b):
    return jnp.dot(a, b, preferred_element_type=F32)


def _dot_nt(a, b):
    return lax.dot_general(a, b, (((1,), (1,)), ((), ())), preferred_element_type=F32)


def _row_spec(tm, n):
    return pl.BlockSpec((tm, n), lambda i: (i, 0))


def _const_spec(shape):
    return pl.BlockSpec(shape, lambda *_: (0,) * len(shape))


def _resident_spec(shape):
    return pl.BlockSpec(shape, lambda *_: (0,) * len(shape), pipeline_mode=pl.Buffered(1))


def _rope_table_kernel(pos_ref, invf_ref, cos_ref, sin_ref):
    ang = pos_ref[...] * invf_ref[...]
    lane = lax.broadcasted_iota(jnp.int32, (1, LANES), 1)
    sign = jnp.where((lane // (MLA_ROPE // 2)) % 2 == 0, -1.0, 1.0)
    cos_ref[...] = jnp.cos(ang)
    sin_ref[...] = jnp.sin(ang) * sign


def _rope_tables(pos_col, invf4):
    T = pos_col.shape[0]
    tm = TM_PROJ
    return pl.pallas_call(
        _rope_table_kernel,
        grid=(T // tm,),
        in_specs=[_row_spec(tm, 1), _const_spec((1, LANES))],
        out_specs=[_row_spec(tm, LANES), _row_spec(tm, LANES)],
        out_shape=[jax.ShapeDtypeStruct((T, LANES), F32)] * 2,
        compiler_params=_cparams("parallel"),
        name="rope_tables",
    )(pos_col, invf4)


def _mla_proj_kernel(x_ref, g_ref, wd_ref, qlg_ref, kvlg_ref, wuq_ref, wukv_ref,
                     qng_ref, qrg_ref, kng_ref, krg_ref, cos_ref, sin_ref, ones_ref,
                     q_ref, kn_ref, kr_ref, v_ref):
    h = _rms(x_ref[...], g_ref[...]).astype(BF16)
    lat = _dot(h, wd_ref[...])
    c_q = _rms(lat[:, :MLA_Q_LORA], qlg_ref[...]).astype(BF16)
    c_kv = _rms(lat[:, MLA_Q_LORA:MLA_Q_LORA + MLA_KV_LORA], kvlg_ref[...]).astype(BF16)
    cos = cos_ref[...]
    sin = sin_ref[...]

    def rope(r):
        return r * cos + pltpu.roll(r, MLA_ROPE // 2, 1) * sin

    ones2 = ones_ref[...]
    ones1 = ones_ref[:LANES, :LANES]

    def group_rms(t, g, ones):
        return t * lax.rsqrt(_dot((t * t).astype(BF16), ones) + EPS) * g

    lane = lax.broadcasted_iota(jnp.int32, (1, LANES), 1)
    kr = rope(group_rms(lat[:, MLA_Q_LORA + MLA_KV_LORA:], krg_ref[...], ones1))
    kr_ref[...] = jnp.where(lane < MLA_ROPE, kr, 0.0).astype(BF16)

    scale = MLA_QK ** -0.5 * LOG2_E
    qg = jnp.concatenate([qng_ref[...], qrg_ref[...]], axis=1) * scale
    kng = kng_ref[...]
    for hh in range(MLA_HEADS):
        q = _dot(c_q, wuq_ref[:, hh * HEAD_PAD:(hh + 1) * HEAD_PAD])
        q = group_rms(q, qg, ones2)
        q_ref[:, hh * HEAD_PAD:hh * HEAD_PAD + LANES] = q[:, :LANES].astype(BF16)
        q_ref[:, hh * HEAD_PAD + LANES:(hh + 1) * HEAD_PAD] = rope(q[:, LANES:]).astype(BF16)
        kv = _dot(c_kv, wukv_ref[:, hh * 2 * LANES:(hh + 1) * 2 * LANES])
        kn_ref[:, hh * LANES:(hh + 1) * LANES] = group_rms(kv[:, :LANES], kng, ones1).astype(BF16)
        v_ref[:, hh * LANES:(hh + 1) * LANES] = kv[:, LANES:].astype(BF16)


def _mla_proj(x2, g, wd, qlg, kvlg, wuq, wukv, qng, qrg, kng, krg, cos, sin):
    T = x2.shape[0]
    tm = TM_PROJ
    group = np.arange(HEAD_PAD) // LANES
    ones = jnp.asarray((group[:, None] == group[None, :]) / LANES, BF16)
    return pl.pallas_call(
        _mla_proj_kernel,
        grid=(T // tm,),
        in_specs=[_row_spec(tm, D_MODEL), _const_spec((1, D_MODEL)), _resident_spec(wd.shape),
                  _const_spec((1, MLA_Q_LORA)), _const_spec((1, MLA_KV_LORA)),
                  _resident_spec(wuq.shape), _resident_spec(wukv.shape),
                  _const_spec((1, LANES)), _const_spec((1, LANES)), _const_spec((1, LANES)),
                  _const_spec((1, LANES)), _row_spec(tm, LANES), _row_spec(tm, LANES),
                  _const_spec((HEAD_PAD, HEAD_PAD))],
        out_specs=[_row_spec(tm, MLA_HEADS * HEAD_PAD), _row_spec(tm, MLA_HEADS * LANES),
                   _row_spec(tm, LANES), _row_spec(tm, MLA_HEADS * LANES)],
        out_shape=[jax.ShapeDtypeStruct((T, MLA_HEADS * HEAD_PAD), BF16),
                   jax.ShapeDtypeStruct((T, MLA_HEADS * LANES), BF16),
                   jax.ShapeDtypeStruct((T, LANES), BF16),
                   jax.ShapeDtypeStruct((T, MLA_HEADS * LANES), BF16)],
        compiler_params=_cparams("parallel"),
        name="mla_proj",
    )(x2, g, wd, qlg, kvlg, wuq, wukv, qng, qrg, kng, krg, cos, sin, ones)


def _attn_kernel(q_ref, kn_ref, kr_ref, v_ref, o_ref, kcat_ref, vt_ref, *, tq):
    n_q = q_ref.shape[1] // tq
    kcat_ref[:, :LANES] = kn_ref[0]
    kcat_ref[:, LANES:] = kr_ref[0]
    vt_ref[...] = v_ref[0].astype(F32).T.astype(BF16)

    def scores(i, j):
        return _dot_nt(kcat_ref[j * tq:(j + 1) * tq, :], q_ref[0, i * tq:(i + 1) * tq, :])

    def accumulate(j, s, carry, masked):
        m, l, acc = carry
        if masked:
            key = lax.broadcasted_iota(jnp.int32, (tq, tq), 0)
            qry = lax.broadcasted_iota(jnp.int32, (tq, tq), 1)
            s = jnp.where(key <= qry, s, -jnp.inf)
        m_new = jnp.maximum(m, jnp.max(s, axis=0, keepdims=True))
        alpha = jnp.exp2(m - m_new)
        p = jnp.exp2(s - m_new)
        l = alpha * l + jnp.sum(p, axis=0, keepdims=True)
        acc = alpha * acc + _dot(vt_ref[:, j * tq:(j + 1) * tq], p.astype(BF16))
        return m_new, l, acc

    pairs = [(i, j) for i in range(n_q) for j in range(i + 1)]
    s_next = scores(*pairs[0])
    carry = None
    for n, (i, j) in enumerate(pairs):
        s, s_next = s_next, (scores(*pairs[n + 1]) if n + 1 < len(pairs) else None)
        if j == 0:
            carry = (jnp.full((1, tq), -jnp.inf, F32), jnp.zeros((1, tq), F32), jnp.zeros((MLA_V, tq), F32))
        carry = accumulate(j, s, carry, masked=(j == i))
        if j == i:
            o_ref[0, i * tq:(i + 1) * tq, :] = (carry[2] / carry[1]).T.astype(BF16)


def _attention(q, kn, kr, v):
    B, S, _ = q.shape
    return pl.pallas_call(
        functools.partial(_attn_kernel, tq=TQ_ATTN),
        grid=(B, MLA_HEADS),
        in_specs=[pl.BlockSpec((1, S, HEAD_PAD), lambda b, h: (b, 0, h)),
                  pl.BlockSpec((1, S, LANES), lambda b, h: (b, 0, h)),
                  pl.BlockSpec((1, S, LANES), lambda b, h: (b, 0, 0)),
                  pl.BlockSpec((1, S, LANES), lambda b, h: (b, 0, h))],
        out_specs=pl.BlockSpec((1, S, MLA_V), lambda b, h: (b, 0, h)),
        out_shape=jax.ShapeDtypeStruct((B, S, MLA_HEADS * MLA_V), BF16),
        scratch_shapes=[pltpu.VMEM((S, HEAD_PAD), BF16), pltpu.VMEM((MLA_V, S), BF16)],
        compiler_params=_cparams("parallel", "parallel"),
        name="mla_attention",
    )(q, kn, kr, v)


def _tail_kernel(x_ref, m_ref, wo_ref, bo_ref, g_ref, win_ref, wout_ref, o_ref):
    x1 = x_ref[...] + _dot(m_ref[...], wo_ref[...]) + bo_ref[...]
    h = _rms(x1, g_ref[...]).astype(BF16)
    acc = x1
    for kk in range(D_FF // TF_TAIL):
        a = jnp.maximum(_dot(h, win_ref[:, kk * TF_TAIL:(kk + 1) * TF_TAIL]), 0.0)
        acc = acc + _dot((a * a).astype(BF16), wout_ref[kk * TF_TAIL:(kk + 1) * TF_TAIL, :])
    o_ref[...] = acc


def _tail(x2, m2, wo, bo, g, win, wout):
    T = x2.shape[0]
    tm = TM_TAIL
    return pl.pallas_call(
        _tail_kernel,
        grid=(T // tm,),
        in_specs=[_row_spec(tm, D_MODEL), _row_spec(tm, D_MODEL), _resident_spec(wo.shape),
                  _const_spec((1, D_MODEL)), _const_spec((1, D_MODEL)),
                  _resident_spec(win.shape), _resident_spec(wout.shape)],
        out_specs=_row_spec(tm, D_MODEL),
        out_shape=jax.ShapeDtypeStruct((T, D_MODEL), F32),
        compiler_params=_cparams("parallel"),
        name="tail_mlp",
    )(x2, m2, wo, bo, g, win, wout)


def _hg_lb_kernel(logits_ref, lb_ref):
    z = logits_ref[...]
    e = jnp.exp(z - jnp.max(z, axis=0, keepdims=True))
    sm = e / jnp.sum(e, axis=0, keepdims=True)
    depth = z.shape[0]
    acc = sm[0:1]
    first = acc
    lb_ref[0:1, :] = acc - first
    for i in range(1, depth):
        acc = acc + sm[i:i + 1]
        lb_ref[i:i + 1, :] = acc - first


def _hg_lb_table(logits):
    return pl.pallas_call(
        _hg_lb_kernel,
        out_shape=jax.ShapeDtypeStruct(logits.shape, F32),
        name="hgrn_lower_bounds",
    )(logits)


def _hg_proj_kernel(x_ref, g_ref, w_ref, lb_ref, q_ref, k_ref, v_ref, lf_ref, gate_ref):
    h = _rms(x_ref[...], g_ref[...]).astype(BF16)
    lb = lb_ref[...]
    n = HG_HEADS * HG_D
    q_ref[...] = _dot(h, w_ref[:, 0:n]).astype(BF16)
    fz = _dot(h, w_ref[:, n:2 * n])
    lf_ref[...] = jnp.log2(lb + (1.0 - lb) * jax.nn.sigmoid(fz))
    k_ref[...] = ((1.0 - lb) * jax.nn.sigmoid(-fz)).astype(BF16)
    v_ref[...] = _dot(h, w_ref[:, 2 * n:3 * n]).astype(BF16)
    gz = _dot(h, w_ref[:, 3 * n:4 * n])
    gate_ref[...] = (gz * jax.nn.sigmoid(gz)).astype(BF16)


def _hg_proj(x2, g, w, lb):
    T = x2.shape[0]
    tm = TM_PROJ
    n = HG_HEADS * HG_D
    bf = jax.ShapeDtypeStruct((T, n), BF16)
    return pl.pallas_call(
        _hg_proj_kernel,
        grid=(T // tm,),
        in_specs=[_row_spec(tm, D_MODEL), _const_spec((1, D_MODEL)), _resident_spec(w.shape),
                  _const_spec((1, n))],
        out_specs=[_row_spec(tm, n)] * 5,
        out_shape=[bf, bf, bf, jax.ShapeDtypeStruct((T, n), F32), bf],
        compiler_params=_cparams("parallel"),
        name="hgrn_proj",
    )(x2, g, w, lb)


HG_LEVELS = 6
HG_COARSE = 3
HG_GROUP = 32
HG_OUT_GROUP = 32


def _hg_level(lvl):
    C = HG_CHUNK
    t = np.arange(C)
    half = (C // 2) >> lvl
    blk = t // (2 * half)
    second = (t % (2 * half)) >= half
    mid = blk * 2 * half + half - 1
    return half, blk, second, mid


def _hg_constants():
    C = HG_CHUNK
    t = np.arange(C)
    j = t[None, :]
    mats = [j <= t[:, None]]
    signs, masks = [], []
    for lvl in range(HG_LEVELS):
        half, blk, second, mid = _hg_level(lvl)
        if lvl < HG_COARSE:
            signs.append(np.where(second, 1.0, -1.0)[:, None] * np.ones((1, HG_D)))
        else:
            a_q = (j > mid[:, None]) & (j <= t[:, None])
            a_k = (j > t[:, None]) & (j <= mid[:, None])
            mats.append(np.where(second[:, None], a_q, a_k))
        masks.append((blk[:, None] == blk[None, :]) & second[:, None] & (~second)[None, :])
    masks.append(np.eye(C, dtype=bool))
    a = np.concatenate(mats, axis=0).astype(np.float32)
    a3 = np.concatenate([a, a, a], axis=1)
    return (jnp.asarray(a3, BF16), jnp.asarray(np.stack(signs), F32),
            jnp.asarray(np.stack(masks).astype(np.float32)))


def _hg_scan_kernel(q_ref, k_ref, v_ref, lf_ref, gate_ref, a3_ref, sign_ref, mask_ref, gout_ref, o_ref,
                    qh_ref, oi_ref, st_ref):
    C = HG_CHUNK
    n_chunks = q_ref.shape[1] // C
    gout = gout_ref[...]

    G = min(HG_GROUP, n_chunks)
    assert n_chunks % G == 0
    grp = range(G)

    def group(gi, state):
        rows = [pl.ds(pl.multiple_of((gi * G + c) * C, C), C) for c in grp]
        lf3 = []
        for c in grp:
            lf = lf_ref[0, rows[c], :]
            hi = lf.astype(BF16)
            r1 = lf - hi.astype(F32)
            mid = r1.astype(BF16)
            lo = (r1 - mid.astype(F32)).astype(BF16)
            lf3.append(jnp.concatenate([hi, mid, lo], axis=0))
        seg = _dot(a3_ref[...], jnp.concatenate(lf3, axis=1))
        seg = [seg[:, c * HG_D:(c + 1) * HG_D] for c in grp]
        b = [seg[c][0:C] for c in grp]
        q = [q_ref[0, rows[c], :].astype(F32) for c in grp]
        k = [k_ref[0, rows[c], :].astype(F32) for c in grp]
        v = [v_ref[0, rows[c], :] for c in grp]
        attn = [_dot_nt(q[c].astype(BF16), k[c].astype(BF16)) * mask_ref[HG_LEVELS] for c in grp]
        upd = [lax.dot_general(v[c], (k[c] * jnp.exp2(b[c][C - 1:C] - b[c])).astype(BF16),
                               (((0,), (0,)), ((), ())), preferred_element_type=F32) for c in grp]
        for c in grp:
            qh_ref[rows[c], :] = (q[c] * jnp.exp2(b[c])).astype(BF16)
        for lvl in range(HG_LEVELS):
            for c in grp:
                if lvl < HG_COARSE:
                    half, _, _, mids = _hg_level(lvl)
                    b_mid = jnp.concatenate([jnp.broadcast_to(b[c][m:m + 1], (2 * half, HG_D))
                                             for m in mids[::2 * half]], axis=0)
                    e = (b[c] - b_mid) * sign_ref[lvl]
                else:
                    e = seg[c][(1 + lvl - HG_COARSE) * C:(2 + lvl - HG_COARSE) * C]
                xl = jnp.exp2(e)
                attn[c] = attn[c] + _dot_nt((q[c] * xl).astype(BF16), (k[c] * xl).astype(BF16)) * mask_ref[lvl]
        for c in grp:
            oi_ref[rows[c], :] = _dot(attn[c].astype(BF16), v[c])
            st_ref[gi * G + c] = state.astype(BF16)
            state = state * jnp.exp2(b[c][C - 1:C]) + upd[c]
        return state

    lax.fori_loop(0, n_chunks // G, group, jnp.zeros((HG_D, HG_D), F32))

    GO = min(HG_OUT_GROUP, n_chunks)
    assert n_chunks % GO == 0
    out_grp = range(GO)

    def out_group(gi, carry):
        rows = [pl.ds(pl.multiple_of((gi * GO + c) * C, C), C) for c in out_grp]
        o = [_dot_nt(qh_ref[rows[c], :], st_ref[gi * GO + c]) + oi_ref[rows[c], :] for c in out_grp]
        o = [o[c] * lax.rsqrt(jnp.mean(o[c] * o[c], axis=-1, keepdims=True) + EPS) * gout for c in out_grp]
        for c in out_grp:
            o_ref[0, rows[c], :] = (o[c] * gate_ref[0, rows[c], :].astype(F32)).astype(BF16)
        return carry

    lax.fori_loop(0, n_chunks // GO, out_group, 0)


def _hg_scan(q, k, v, lf, gate, gout):
    B, S, _ = q.shape
    a3, signs, masks = _hg_constants()
    blk = pl.BlockSpec((1, S, HG_D), lambda b, h: (b, 0, h))
    return pl.pallas_call(
        _hg_scan_kernel,
        grid=(B, HG_HEADS),
        in_specs=[blk, blk, blk, blk, blk, _const_spec(a3.shape), _const_spec(signs.shape),
                  _const_spec(masks.shape), _const_spec((1, HG_D))],
        out_specs=blk,
        out_shape=jax.ShapeDtypeStruct((B, S, HG_HEADS * HG_D), BF16),
        scratch_shapes=[pltpu.VMEM((S, HG_D), BF16),
                        pltpu.VMEM((S, HG_D), F32),
                        pltpu.VMEM((S // HG_CHUNK, HG_D, HG_D), BF16)],
        compiler_params=_cparams("parallel", "parallel"),
        name="hgrn_scan",
    )(q, k, v, lf, gate, a3, signs, masks, gout)


def _cv_glu_kernel(x_ref, g_ref, w_ref, b_ref, u_ref):
    h = _rms(x_ref[...], g_ref[...]).astype(BF16)
    a = _dot(h, w_ref[:, :D_MODEL]) + b_ref[:, :D_MODEL]
    gate = _dot(h, w_ref[:, D_MODEL:]) + b_ref[:, D_MODEL:]
    u_ref[...] = a * jax.nn.sigmoid(gate)


def _cv_glu(x2, g, w, b):
    T = x2.shape[0]
    tm = TM_PROJ
    return pl.pallas_call(
        _cv_glu_kernel,
        grid=(T // tm,),
        in_specs=[_row_spec(tm, D_MODEL), _const_spec((1, D_MODEL)), _resident_spec(w.shape),
                  _const_spec((1, 2 * D_MODEL))],
        out_specs=_row_spec(tm, D_MODEL),
        out_shape=jax.ShapeDtypeStruct((T, D_MODEL), F32),
        compiler_params=_cparams("parallel"),
        name="conv_glu",
    )(x2, g, w, b)


def _cv_dw_kernel(u_ref, prev_ref, w_ref, b_ref, lng_ref, lnb_ref, o_ref, buf_ref, sh_ref, acc_ref, *, tm):
    i = pl.program_id(1)
    halo = prev_ref[0]
    buf_ref[0:CONV_HALO, :] = jnp.where(i == 0, 0.0, halo)
    buf_ref[CONV_HALO:, :] = u_ref[0]
    n_sh = sh_ref.shape[1]
    for r in range(1, SUBLANES):
        sh_ref[r - 1] = buf_ref[r:r + n_sh, :]
    off = CONV_HALO - (CONV_WIDTH - 1)
    rb = CONV_ROWS
    for ct in range(D_MODEL // LANES):
        cols = slice(ct * LANES, (ct + 1) * LANES)
        for j0 in range(0, CONV_WIDTH, CONV_TAP_GROUP):
            group = range(j0, min(j0 + CONV_TAP_GROUP, CONV_WIDTH))
            taps = {j: w_ref[j:j + 1, cols] for j in group}
            for base in range(0, tm, rb):
                acc = jnp.zeros((rb, LANES), F32) + b_ref[:, cols] if j0 == 0 else acc_ref[base:base + rb, cols]
                for j in group:
                    a, r = divmod(off + j, SUBLANES)
                    lo = base + a * SUBLANES
                    src = buf_ref[lo:lo + rb, cols] if r == 0 else sh_ref[r - 1, lo:lo + rb, cols]
                    acc = acc + src * taps[j]
                acc_ref[base:base + rb, cols] = acc
    acc = acc_ref[...]
    mu = jnp.mean(acc, axis=-1, keepdims=True)
    xc = acc - mu
    y = xc * lax.rsqrt(jnp.mean(xc * xc, axis=-1, keepdims=True) + EPS) * lng_ref[...] + lnb_ref[...]
    o_ref[0] = (y * jax.nn.sigmoid(y)).astype(BF16)


def _cv_dw(u, w, b, lng, lnb):
    B, S, _ = u.shape
    tm = TM_CONV
    ratio = tm // CONV_HALO
    return pl.pallas_call(
        functools.partial(_cv_dw_kernel, tm=tm),
        grid=(B, S // tm),
        in_specs=[pl.BlockSpec((1, tm, D_MODEL), lambda b, i: (b, i, 0)),
                  pl.BlockSpec((1, CONV_HALO, D_MODEL), lambda b, i: (b, jnp.maximum(i * ratio - 1, 0), 0)),
                  _const_spec((CONV_WIDTH, D_MODEL)), _const_spec((1, D_MODEL)),
                  _const_spec((1, D_MODEL)), _const_spec((1, D_MODEL))],
        out_specs=pl.BlockSpec((1, tm, D_MODEL), lambda b, i: (b, i, 0)),
        out_shape=jax.ShapeDtypeStruct((B, S, D_MODEL), BF16),
        scratch_shapes=[pltpu.VMEM((tm + CONV_HALO, D_MODEL), F32),
                        pltpu.VMEM((SUBLANES - 1, tm + CONV_HALO - SUBLANES, D_MODEL), F32),
                        pltpu.VMEM((tm, D_MODEL), F32)],
        compiler_params=_cparams("parallel", "parallel"),
        name="conv_depthwise",
    )(u, u, w, b, lng, lnb)


def _dup_rope(t):
    return jnp.concatenate([t, t], axis=-1)


def _mla_weights(w_down, w_uq, q_head_norm, k_head_norm):
    wd = jnp.concatenate([w_down, w_down[:, -MLA_ROPE:]], axis=1).astype(BF16)
    wq = w_uq.reshape(MLA_Q_LORA, MLA_HEADS, MLA_QK)
    wq = jnp.concatenate([wq, wq[:, :, MLA_NOPE:]], axis=-1)
    wq = wq.reshape(MLA_Q_LORA, MLA_HEADS * HEAD_PAD).astype(BF16)
    qng = q_head_norm[None, :MLA_NOPE]
    qrg = _dup_rope(q_head_norm[None, MLA_NOPE:])
    kng = k_head_norm[None, :MLA_NOPE]
    krg = _dup_rope(k_head_norm[None, MLA_NOPE:])
    return wd, wq, qng, qrg, kng, krg


def kernel(x, positions, norm_mix, norm_mlp, mlp_w_in, mlp_w_out, mla_w_down, mla_q_lat_norm, mla_kv_lat_norm, mla_w_uq, mla_w_ukv, mla_q_head_norm, mla_k_head_norm, mla_w_o, hg_w_in, hg_lb_logits, hg_out_norm, hg_w_o, cv_w_pw1, cv_b_pw1, cv_w_dw, cv_b_dw, cv_ln_g, cv_ln_b, cv_w_pw2, cv_b_pw2):
    B, S, D = x.shape
    T = B * S
    depth = norm_mix.shape[0]
    x2 = x.reshape(T, D)

    inv_freq = jnp.power(ROPE_THETA, -jnp.arange(0, MLA_ROPE, 2, dtype=F32) / MLA_ROPE)
    invf4 = jnp.tile(inv_freq, 4)[None, :]
    cos, sin = _rope_tables(positions.astype(F32).reshape(T, 1), invf4)
    lb_table = _hg_lb_table(hg_lb_logits.astype(F32))
    zero_bias = jnp.zeros((1, D), F32)

    for layer in range(depth):
        kind, j = layer % N_MIXERS, layer // N_MIXERS
        g_mix = norm_mix[layer][None, :]
        if kind == 0:
            wd, wq, qng, qrg, kng, krg = _mla_weights(mla_w_down[j], mla_w_uq[j], mla_q_head_norm[j],
                                                      mla_k_head_norm[j])
            q, kn, kr, v = _mla_proj(x2, g_mix, wd, mla_q_lat_norm[j][None, :], mla_kv_lat_norm[j][None, :],
                                     wq, mla_w_ukv[j].astype(BF16), qng, qrg, kng, krg, cos, sin)
            m = _attention(q.reshape(B, S, -1), kn.reshape(B, S, -1), kr.reshape(B, S, -1),
                           v.reshape(B, S, -1)).reshape(T, D)
            w_o, b_o = mla_w_o[j], zero_bias
        elif kind == 1:
            q, k, v, lf, gate = _hg_proj(x2, g_mix, hg_w_in[j].astype(BF16), lb_table[layer][None, :])
            r3 = lambda t: t.reshape(B, S, -1)
            m = _hg_scan(r3(q), r3(k), r3(v), r3(lf), r3(gate), hg_out_norm[j][None, :]).reshape(T, D)
            w_o, b_o = hg_w_o[j], zero_bias
        else:
            u = _cv_glu(x2, g_mix, cv_w_pw1[j].astype(BF16), cv_b_pw1[j][None, :])
            m = _cv_dw(u.reshape(B, S, D), cv_w_dw[j], cv_b_dw[j][None, :], cv_ln_g[j][None, :],
                       cv_ln_b[j][None, :]).reshape(T, D)
            w_o, b_o = cv_w_pw2[j], cv_b_pw2[j][None, :]
        x2 = _tail(x2, m, w_o.astype(BF16), b_o, norm_mlp[layer][None, :],
                   mlp_w_in[layer].astype(BF16), mlp_w_out[layer].astype(BF16))
    return x2.reshape(B, S, D)
```

```python
import functools

import numpy as np
import jax
import jax.numpy as jnp
from jax import lax
from jax.experimental import pallas as pl
from jax.experimental.pallas import tpu as pltpu

F32 = jnp.float32
BF16 = jnp.bfloat16

D_MODEL = 1024
N_MIXERS = 3
MLA_HEADS = 8
MLA_Q_LORA = 384
MLA_KV_LORA = 256
MLA_NOPE = 128
MLA_ROPE = 64
MLA_V = 128
MLA_QK = MLA_NOPE + MLA_ROPE
ROPE_THETA = 10000.0
HG_HEADS = 8
HG_D = D_MODEL // HG_HEADS
HG_CHUNK = 64
CONV_WIDTH = 31
D_FF = 4 * D_MODEL
EPS = 1e-6
LOG2_E = 1.4426950408889634

LANES = 128
HEAD_PAD = 2 * LANES
VMEM_LIMIT = 56 * 1024 * 1024

TM_PROJ = 512
TM_TAIL = 512
TF_TAIL = 512
TQ_ATTN = 512
ATTN_ONES_ROWS = 16
TM_CONV = 512
CONV_HALO = 32
CONV_ROWS = 64
CONV_TAP_GROUP = 16
SUBLANES = 8


def _cparams(*sem):
    return pltpu.CompilerParams(dimension_semantics=sem, vmem_limit_bytes=VMEM_LIMIT)


def _rms(x, g):
    return x * lax.rsqrt(jnp.mean(x * x, axis=-1, keepdims=True) + EPS) * g


def _dot(a, b):
    return jnp.dot(a, b, preferred_element_type=F32)


def _dot_nt(a, b):
    return lax.dot_general(a, b, (((1,), (1,)), ((), ())), preferred_element_type=F32)


def _row_spec(tm, n):
    return pl.BlockSpec((tm, n), lambda i: (i, 0))


def _const_spec(shape):
    return pl.BlockSpec(shape, lambda *_: (0,) * len(shape))


def _resident_spec(shape):
    return pl.BlockSpec(shape, lambda *_: (0,) * len(shape), pipeline_mode=pl.Buffered(1))


def _rope_table_kernel(pos_ref, invf_ref, cos_ref, sin_ref):
    ang = pos_ref[...] * invf_ref[...]
    lane = lax.broadcasted_iota(jnp.int32, (1, LANES), 1)
    sign = jnp.where((lane // (MLA_ROPE // 2)) % 2 == 0, -1.0, 1.0)
    cos_ref[...] = jnp.cos(ang)
    sin_ref[...] = jnp.sin(ang) * sign


def _rope_tables(pos_col, invf4):
    T = pos_col.shape[0]
    tm = TM_PROJ
    return pl.pallas_call(
        _rope_table_kernel,
        grid=(T // tm,),
        in_specs=[_row_spec(tm, 1), _const_spec((1, LANES))],
        out_specs=[_row_spec(tm, LANES), _row_spec(tm, LANES)],
        out_shape=[jax.ShapeDtypeStruct((T, LANES), F32)] * 2,
        compiler_params=_cparams("parallel"),
        name="rope_tables",
    )(pos_col, invf4)


def _mla_proj_kernel(x_ref, g_ref, wd_ref, qlg_ref, kvlg_ref, wuq_ref, wukv_ref,
                     qng_ref, qrg_ref, kng_ref, krg_ref, cos_ref, sin_ref, ones_ref,
                     q_ref, kn_ref, kr_ref, v_ref):
    h = _rms(x_ref[...], g_ref[...]).astype(BF16)
    lat = _dot(h, wd_ref[...])
    c_q = _rms(lat[:, :MLA_Q_LORA], qlg_ref[...]).astype(BF16)
    c_kv = _rms(lat[:, MLA_Q_LORA:MLA_Q_LORA + MLA_KV_LORA], kvlg_ref[...]).astype(BF16)
    cos = cos_ref[...]
    sin = sin_ref[...]

    def rope(r):
        return r * cos + pltpu.roll(r, MLA_ROPE // 2, 1) * sin

    ones2 = ones_ref[...]
    ones1 = ones_ref[:LANES, :LANES]

    def group_rms(t, g, ones):
        return t * lax.rsqrt(_dot((t * t).astype(BF16), ones) + EPS) * g

    lane = lax.broadcasted_iota(jnp.int32, (1, LANES), 1)
    kr = rope(group_rms(lat[:, MLA_Q_LORA + MLA_KV_LORA:], krg_ref[...], ones1))
    kr_ref[...] = jnp.where(lane < MLA_ROPE, kr, 0.0).astype(BF16)

    scale = MLA_QK ** -0.5 * LOG2_E
    qg = jnp.concatenate([qng_ref[...], qrg_ref[...]], axis=1) * scale
    kng2 = jnp.concatenate([kng_ref[...], kng_ref[...]], axis=1)
    heads = range(MLA_HEADS)
    q = [_dot(c_q, wuq_ref[:, hh * HEAD_PAD:(hh + 1) * HEAD_PAD]) for hh in heads]
    kv = [_dot(c_kv, wukv_ref[:, hh * 2 * LANES:(hh + 1) * 2 * LANES]) for hh in heads]
    for hh in heads:
        v_ref[:, hh * LANES:(hh + 1) * LANES] = kv[hh][:, LANES:].astype(BF16)
    q = [group_rms(q[hh], qg, ones2) for hh in heads]
    for hh in range(0, MLA_HEADS, 2):
        kn = jnp.concatenate([kv[hh][:, :LANES], kv[hh + 1][:, :LANES]], axis=1)
        kn_ref[:, hh * LANES:(hh + 2) * LANES] = group_rms(kn, kng2, ones2).astype(BF16)
    for hh in heads:
        q_ref[:, hh * HEAD_PAD:hh * HEAD_PAD + LANES] = q[hh][:, :LANES].astype(BF16)
        q_ref[:, hh * HEAD_PAD + LANES:(hh + 1) * HEAD_PAD] = rope(q[hh][:, LANES:]).astype(BF16)


def _mla_proj(x2, g, wd, qlg, kvlg, wuq, wukv, qng, qrg, kng, krg, cos, sin):
    T = x2.shape[0]
    tm = TM_PROJ
    group = np.arange(HEAD_PAD) // LANES
    ones = jnp.asarray((group[:, None] == group[None, :]) / LANES, BF16)
    return pl.pallas_call(
        _mla_proj_kernel,
        grid=(T // tm,),
        in_specs=[_row_spec(tm, D_MODEL), _const_spec((1, D_MODEL)), _resident_spec(wd.shape),
                  _const_spec((1, MLA_Q_LORA)), _const_spec((1, MLA_KV_LORA)),
                  _resident_spec(wuq.shape), _resident_spec(wukv.shape),
                  _const_spec((1, LANES)), _const_spec((1, LANES)), _const_spec((1, LANES)),
                  _const_spec((1, LANES)), _row_spec(tm, LANES), _row_spec(tm, LANES),
                  _const_spec((HEAD_PAD, HEAD_PAD))],
        out_specs=[_row_spec(tm, MLA_HEADS * HEAD_PAD), _row_spec(tm, MLA_HEADS * LANES),
                   _row_spec(tm, LANES), _row_spec(tm, MLA_HEADS * LANES)],
        out_shape=[jax.ShapeDtypeStruct((T, MLA_HEADS * HEAD_PAD), BF16),
                   jax.ShapeDtypeStruct((T, MLA_HEADS * LANES), BF16),
                   jax.ShapeDtypeStruct((T, LANES), BF16),
                   jax.ShapeDtypeStruct((T, MLA_HEADS * LANES), BF16)],
        compiler_params=_cparams("parallel"),
        name="mla_proj",
    )(x2, g, wd, qlg, kvlg, wuq, wukv, qng, qrg, kng, krg, cos, sin, ones)


def _attn_kernel(q_ref, kn_ref, kr_ref, v_ref, o_ref, kcat_ref, vt_ref, *, tq):
    n_q = q_ref.shape[1] // tq
    kcat_ref[:, :LANES] = kn_ref[0]
    kcat_ref[:, LANES:] = kr_ref[0]
    vt_ref[:MLA_V, :] = v_ref[0].astype(F32).T.astype(BF16)
    vt_ref[MLA_V:, :] = jnp.ones((ATTN_ONES_ROWS, vt_ref.shape[1]), BF16)

    def scores(i, j):
        return _dot_nt(kcat_ref[j * tq:(j + 1) * tq, :], q_ref[0, i * tq:(i + 1) * tq, :])

    def accumulate(j, s, carry, masked):
        m, acc = carry
        if masked:
            key = lax.broadcasted_iota(jnp.int32, (tq, tq), 0)
            qry = lax.broadcasted_iota(jnp.int32, (tq, tq), 1)
            s = jnp.where(key <= qry, s, -jnp.inf)
        m_new = jnp.maximum(m, jnp.max(s, axis=0, keepdims=True))
        p = jnp.exp2(s - m_new)
        acc = jnp.exp2(m - m_new) * acc + _dot(vt_ref[:, j * tq:(j + 1) * tq], p.astype(BF16))
        return m_new, acc

    pairs = [(i, j) for i in range(n_q) for j in range(i + 1)]
    s_next = scores(*pairs[0])
    carry = None
    for n, (i, j) in enumerate(pairs):
        s, s_next = s_next, (scores(*pairs[n + 1]) if n + 1 < len(pairs) else None)
        if j == 0:
            carry = (jnp.full((1, tq), -jnp.inf, F32), jnp.zeros((MLA_V + ATTN_ONES_ROWS, tq), F32))
        carry = accumulate(j, s, carry, masked=(j == i))
        if j == i:
            acc = carry[1]
            o_ref[0, i * tq:(i + 1) * tq, :] = (acc[:MLA_V] / acc[MLA_V:MLA_V + 1]).T.astype(BF16)


def _attention(q, kn, kr, v):
    B, S, _ = q.shape
    return pl.pallas_call(
        functools.partial(_attn_kernel, tq=TQ_ATTN),
        grid=(B, MLA_HEADS),
        in_specs=[pl.BlockSpec((1, S, HEAD_PAD), lambda b, h: (b, 0, h)),
                  pl.BlockSpec((1, S, LANES), lambda b, h: (b, 0, h)),
                  pl.BlockSpec((1, S, LANES), lambda b, h: (b, 0, 0)),
                  pl.BlockSpec((1, S, LANES), lambda b, h: (b, 0, h))],
        out_specs=pl.BlockSpec((1, S, MLA_V), lambda b, h: (b, 0, h)),
        out_shape=jax.ShapeDtypeStruct((B, S, MLA_HEADS * MLA_V), BF16),
        scratch_shapes=[pltpu.VMEM((S, HEAD_PAD), BF16), pltpu.VMEM((MLA_V + ATTN_ONES_ROWS, S), BF16)],
        compiler_params=_cparams("parallel", "parallel"),
        name="mla_attention",
    )(q, kn, kr, v)


def _tail_kernel(x_ref, m_ref, wo_ref, bo_ref, g_ref, win_ref, wout_ref, o_ref):
    x1 = x_ref[...] + _dot(m_ref[...], wo_ref[...]) + bo_ref[...]
    h = _rms(x1, g_ref[...]).astype(BF16)
    acc = x1
    for kk in range(D_FF // TF_TAIL):
        a = jnp.maximum(_dot(h, win_ref[:, kk * TF_TAIL:(kk + 1) * TF_TAIL]), 0.0)
        acc = acc + _dot((a * a).astype(BF16), wout_ref[kk * TF_TAIL:(kk + 1) * TF_TAIL, :])
    o_ref[...] = acc


def _tail(x2, m2, wo, bo, g, win, wout):
    T = x2.shape[0]
    tm = TM_TAIL
    return pl.pallas_call(
        _tail_kernel,
        grid=(T // tm,),
        in_specs=[_row_spec(tm, D_MODEL), _row_spec(tm, D_MODEL), _resident_spec(wo.shape),
                  _const_spec((1, D_MODEL)), _const_spec((1, D_MODEL)),
                  _resident_spec(win.shape), _resident_spec(wout.shape)],
        out_specs=_row_spec(tm, D_MODEL),
        out_shape=jax.ShapeDtypeStruct((T, D_MODEL), F32),
        compiler_params=_cparams("parallel"),
        name="tail_mlp",
    )(x2, m2, wo, bo, g, win, wout)


def _hg_lb_kernel(logits_ref, lb_ref):
    z = logits_ref[...]
    e = jnp.exp(z - jnp.max(z, axis=0, keepdims=True))
    sm = e / jnp.sum(e, axis=0, keepdims=True)
    depth = z.shape[0]
    acc = sm[0:1]
    first = acc
    lb_ref[0:1, :] = acc - first
    for i in range(1, depth):
        acc = acc + sm[i:i + 1]
        lb_ref[i:i + 1, :] = acc - first


def _hg_lb_table(logits):
    return pl.pallas_call(
        _hg_lb_kernel,
        out_shape=jax.ShapeDtypeStruct(logits.shape, F32),
        name="hgrn_lower_bounds",
    )(logits)


def _hg_proj_kernel(x_ref, g_ref, w_ref, lb_ref, q_ref, k_ref, v_ref, lf_ref, gate_ref):
    h = _rms(x_ref[...], g_ref[...]).astype(BF16)
    lb = lb_ref[...]
    n = HG_HEADS * HG_D
    fz = _dot(h, w_ref[:, n:2 * n])
    gz = _dot(h, w_ref[:, 3 * n:4 * n])
    q_ref[...] = _dot(h, w_ref[:, 0:n]).astype(BF16)
    lf_ref[...] = jnp.log2(lb + (1.0 - lb) * jax.nn.sigmoid(fz))
    k_ref[...] = ((1.0 - lb) * jax.nn.sigmoid(-fz)).astype(BF16)
    v_ref[...] = _dot(h, w_ref[:, 2 * n:3 * n]).astype(BF16)
    gate_ref[...] = (gz * jax.nn.sigmoid(gz)).astype(BF16)


def _hg_proj(x2, g, w, lb):
    T = x2.shape[0]
    tm = TM_PROJ
    n = HG_HEADS * HG_D
    bf = jax.ShapeDtypeStruct((T, n), BF16)
    return pl.pallas_call(
        _hg_proj_kernel,
        grid=(T // tm,),
        in_specs=[_row_spec(tm, D_MODEL), _const_spec((1, D_MODEL)), _resident_spec(w.shape),
                  _const_spec((1, n))],
        out_specs=[_row_spec(tm, n)] * 5,
        out_shape=[bf, bf, bf, jax.ShapeDtypeStruct((T, n), F32), bf],
        compiler_params=_cparams("parallel"),
        name="hgrn_proj",
    )(x2, g, w, lb)


HG_LEVELS = 6
HG_COARSE = 3
HG_GROUP = 32
HG_OUT_GROUP = 32


def _hg_level(lvl):
    C = HG_CHUNK
    t = np.arange(C)
    half = (C // 2) >> lvl
    blk = t // (2 * half)
    second = (t % (2 * half)) >= half
    mid = blk * 2 * half + half - 1
    return half, blk, second, mid


def _hg_constants():
    C = HG_CHUNK
    t = np.arange(C)
    j = t[None, :]
    mats = [j <= t[:, None]]
    signs, masks = [], []
    for lvl in range(HG_LEVELS):
        half, blk, second, mid = _hg_level(lvl)
        if lvl < HG_COARSE:
            signs.append(np.where(second, 1.0, -1.0)[:, None] * np.ones((1, HG_D)))
        else:
            a_q = (j > mid[:, None]) & (j <= t[:, None])
            a_k = (j > t[:, None]) & (j <= mid[:, None])
            mats.append(np.where(second[:, None], a_q, a_k))
        masks.append((blk[:, None] == blk[None, :]) & second[:, None] & (~second)[None, :])
    masks.append(np.eye(C, dtype=bool))
    a = np.concatenate(mats, axis=0).astype(np.float32)
    a3 = np.concatenate([a, a, a], axis=1)
    return (jnp.asarray(a3, BF16), jnp.asarray(np.stack(signs), F32),
            jnp.asarray(np.stack(masks).astype(np.float32)))


def _hg_scan_kernel(q_ref, k_ref, v_ref, lf_ref, gate_ref, a3_ref, sign_ref, mask_ref, gout_ref, o_ref,
                    qh_ref, oi_ref, st_ref):
    C = HG_CHUNK
    n_chunks = q_ref.shape[1] // C
    gout = gout_ref[...]

    G = min(HG_GROUP, n_chunks)
    assert n_chunks % G == 0
    grp = range(G)

    def group(gi, state):
        rows = [pl.ds(pl.multiple_of((gi * G + c) * C, C), C) for c in grp]
        lf3 = []
        for c in grp:
            lf = lf_ref[0, rows[c], :]
            hi = lf.astype(BF16)
            r1 = lf - hi.astype(F32)
            mid = r1.astype(BF16)
            lo = (r1 - mid.astype(F32)).astype(BF16)
            lf3.append(jnp.concatenate([hi, mid, lo], axis=0))
        seg = _dot(a3_ref[...], jnp.concatenate(lf3, axis=1))
        seg = [seg[:, c * HG_D:(c + 1) * HG_D] for c in grp]
        b = [seg[c][0:C] for c in grp]
        q = [q_ref[0, rows[c], :].astype(F32) for c in grp]
        k = [k_ref[0, rows[c], :].astype(F32) for c in grp]
        v = [v_ref[0, rows[c], :] for c in grp]
        attn = [_dot_nt(q[c].astype(BF16), k[c].astype(BF16)) * mask_ref[HG_LEVELS] for c in grp]
        upd = [lax.dot_general(v[c], (k[c] * jnp.exp2(b[c][C - 1:C] - b[c])).astype(BF16),
                               (((0,), (0,)), ((), ())), preferred_element_type=F32) for c in grp]
        for c in grp:
            qh_ref[rows[c], :] = (q[c] * jnp.exp2(b[c])).astype(BF16)
        for lvl in range(HG_LEVELS):
            for c in grp:
                if lvl < HG_COARSE:
                    half, _, _, mids = _hg_level(lvl)
                    b_mid = jnp.concatenate([jnp.broadcast_to(b[c][m:m + 1], (2 * half, HG_D))
                                             for m in mids[::2 * half]], axis=0)
                    e = (b[c] - b_mid) * sign_ref[lvl]
                else:
                    e = seg[c][(1 + lvl - HG_COARSE) * C:(2 + lvl - HG_COARSE) * C]
                xl = jnp.exp2(e)
                attn[c] = attn[c] + _dot_nt((q[c] * xl).astype(BF16), (k[c] * xl).astype(BF16)) * mask_ref[lvl]
        for c in grp:
            oi_ref[rows[c], :] = _dot(attn[c].astype(BF16), v[c])
            st_ref[gi * G + c] = state.astype(BF16)
            state = state * jnp.exp2(b[c][C - 1:C]) + upd[c]
        return state

    lax.fori_loop(0, n_chunks // G, group, jnp.zeros((HG_D, HG_D), F32))

    GO = min(HG_OUT_GROUP, n_chunks)
    assert n_chunks % GO == 0
    out_grp = range(GO)

    def out_group(gi, carry):
        rows = [pl.ds(pl.multiple_of((gi * GO + c) * C, C), C) for c in out_grp]
        o = [_dot_nt(qh_ref[rows[c], :], st_ref[gi * GO + c]) + oi_ref[rows[c], :] for c in out_grp]
        o = [o[c] * lax.rsqrt(jnp.mean(o[c] * o[c], axis=-1, keepdims=True) + EPS) * gout for c in out_grp]
        for c in out_grp:
            o_ref[0, rows[c], :] = (o[c] * gate_ref[0, rows[c], :].astype(F32)).astype(BF16)
        return carry

    lax.fori_loop(0, n_chunks // GO, out_group, 0)


def _hg_scan(q, k, v, lf, gate, gout):
    B, S, _ = q.shape
    a3, signs, masks = _hg_constants()
    blk = pl.BlockSpec((1, S, HG_D), lambda b, h: (b, 0, h))
    return pl.pallas_call(
        _hg_scan_kernel,
        grid=(B, HG_HEADS),
        in_specs=[blk, blk, blk, blk, blk, _const_spec(a3.shape), _const_spec(signs.shape),
                  _const_spec(masks.shape), _const_spec((1, HG_D))],
        out_specs=blk,
        out_shape=jax.ShapeDtypeStruct((B, S, HG_HEADS * HG_D), BF16),
        scratch_shapes=[pltpu.VMEM((S, HG_D), BF16),
                        pltpu.VMEM((S, HG_D), F32),
                        pltpu.VMEM((S // HG_CHUNK, HG_D, HG_D), BF16)],
        compiler_params=_cparams("parallel", "parallel"),
        name="hgrn_scan",
    )(q, k, v, lf, gate, a3, signs, masks, gout)


def _cv_glu_kernel(x_ref, g_ref, w_ref, b_ref, u_ref):
    h = _rms(x_ref[...], g_ref[...]).astype(BF16)
    a = _dot(h, w_ref[:, :D_MODEL]) + b_ref[:, :D_MODEL]
    gate = _dot(h, w_ref[:, D_MODEL:]) + b_ref[:, D_MODEL:]
    u_ref[...] = a * jax.nn.sigmoid(gate)


def _cv_glu(x2, g, w, b):
    T = x2.shape[0]
    tm = TM_PROJ
    return pl.pallas_call(
        _cv_glu_kernel,
        grid=(T // tm,),
        in_specs=[_row_spec(tm, D_MODEL), _const_spec((1, D_MODEL)), _resident_spec(w.shape),
                  _const_spec((1, 2 * D_MODEL))],
        out_specs=_row_spec(tm, D_MODEL),
        out_shape=jax.ShapeDtypeStruct((T, D_MODEL), F32),
        compiler_params=_cparams("parallel"),
        name="conv_glu",
    )(x2, g, w, b)


def _cv_dw_kernel(u_ref, prev_ref, w_ref, b_ref, lng_ref, lnb_ref, o_ref, buf_ref, sh_ref, acc_ref, *, tm):
    i = pl.program_id(1)
    halo = prev_ref[0]
    buf_ref[0:CONV_HALO, :] = jnp.where(i == 0, 0.0, halo)
    buf_ref[CONV_HALO:, :] = u_ref[0]
    n_sh = sh_ref.shape[1]
    for r in range(1, SUBLANES):
        sh_ref[r - 1] = buf_ref[r:r + n_sh, :]
    off = CONV_HALO - (CONV_WIDTH - 1)
    rb = CONV_ROWS
    for ct in range(D_MODEL // LANES):
        cols = slice(ct * LANES, (ct + 1) * LANES)
        for j0 in range(0, CONV_WIDTH, CONV_TAP_GROUP):
            group = range(j0, min(j0 + CONV_TAP_GROUP, CONV_WIDTH))
            taps = {j: w_ref[j:j + 1, cols] for j in group}
            for base in range(0, tm, rb):
                acc = jnp.zeros((rb, LANES), F32) + b_ref[:, cols] if j0 == 0 else acc_ref[base:base + rb, cols]
                for j in group:
                    a, r = divmod(off + j, SUBLANES)
                    lo = base + a * SUBLANES
                    src = buf_ref[lo:lo + rb, cols] if r == 0 else sh_ref[r - 1, lo:lo + rb, cols]
                    acc = acc + src * taps[j]
                acc_ref[base:base + rb, cols] = acc
    acc = acc_ref[...]
    mu = jnp.mean(acc, axis=-1, keepdims=True)
    xc = acc - mu
    y = xc * lax.rsqrt(jnp.mean(xc * xc, axis=-1, keepdims=True) + EPS) * lng_ref[...] + lnb_ref[...]
    o_ref[0] = (y * jax.nn.sigmoid(y)).astype(BF16)


def _cv_dw(u, w, b, lng, lnb):
    B, S, _ = u.shape
    tm = TM_CONV
    ratio = tm // CONV_HALO
    return pl.pallas_call(
        functools.partial(_cv_dw_kernel, tm=tm),
        grid=(B, S // tm),
        in_specs=[pl.BlockSpec((1, tm, D_MODEL), lambda b, i: (b, i, 0)),
                  pl.BlockSpec((1, CONV_HALO, D_MODEL), lambda b, i: (b, jnp.maximum(i * ratio - 1, 0), 0)),
                  _const_spec((CONV_WIDTH, D_MODEL)), _const_spec((1, D_MODEL)),
                  _const_spec((1, D_MODEL)), _const_spec((1, D_MODEL))],
        out_specs=pl.BlockSpec((1, tm, D_MODEL), lambda b, i: (b, i, 0)),
        out_shape=jax.ShapeDtypeStruct((B, S, D_MODEL), BF16),
        scratch_shapes=[pltpu.VMEM((tm + CONV_HALO, D_MODEL), F32),
                        pltpu.VMEM((SUBLANES - 1, tm + CONV_HALO - SUBLANES, D_MODEL), F32),
                        pltpu.VMEM((tm, D_MODEL), F32)],
        compiler_params=_cparams("parallel", "parallel"),
        name="conv_depthwise",
    )(u, u, w, b, lng, lnb)


def _dup_rope(t):
    return jnp.concatenate([t, t], axis=-1)


def _mla_weights(w_down, w_uq, q_head_norm, k_head_norm):
    wd = jnp.concatenate([w_down, w_down[:, -MLA_ROPE:]], axis=1).astype(BF16)
    wq = w_uq.reshape(MLA_Q_LORA, MLA_HEADS, MLA_QK)
    wq = jnp.concatenate([wq, wq[:, :, MLA_NOPE:]], axis=-1)
    wq = wq.reshape(MLA_Q_LORA, MLA_HEADS * HEAD_PAD).astype(BF16)
    qng = q_head_norm[None, :MLA_NOPE]
    qrg = _dup_rope(q_head_norm[None, MLA_NOPE:])
    kng = k_head_norm[None, :MLA_NOPE]
    krg = _dup_rope(k_head_norm[None, MLA_NOPE:])
    return wd, wq, qng, qrg, kng, krg


def kernel(x, positions, norm_mix, norm_mlp, mlp_w_in, mlp_w_out, mla_w_down, mla_q_lat_norm, mla_kv_lat_norm, mla_w_uq, mla_w_ukv, mla_q_head_norm, mla_k_head_norm, mla_w_o, hg_w_in, hg_lb_logits, hg_out_norm, hg_w_o, cv_w_pw1, cv_b_pw1, cv_w_dw, cv_b_dw, cv_ln_g, cv_ln_b, cv_w_pw2, cv_b_pw2):
    B, S, D = x.shape
    T = B * S
    depth = norm_mix.shape[0]
    x2 = x.reshape(T, D)

    inv_freq = jnp.power(ROPE_THETA, -jnp.arange(0, MLA_ROPE, 2, dtype=F32) / MLA_ROPE)
    invf4 = jnp.tile(inv_freq, 4)[None, :]
    cos, sin = _rope_tables(positions.astype(F32).reshape(T, 1), invf4)
    lb_table = _hg_lb_table(hg_lb_logits.astype(F32))
    zero_bias = jnp.zeros((1, D), F32)

    for layer in range(depth):
        kind, j = layer % N_MIXERS, layer // N_MIXERS
        g_mix = norm_mix[layer][None, :]
        if kind == 0:
            wd, wq, qng, qrg, kng, krg = _mla_weights(mla_w_down[j], mla_w_uq[j], mla_q_head_norm[j],
                                                      mla_k_head_norm[j])
            q, kn, kr, v = _mla_proj(x2, g_mix, wd, mla_q_lat_norm[j][None, :], mla_kv_lat_norm[j][None, :],
                                     wq, mla_w_ukv[j].astype(BF16), qng, qrg, kng, krg, cos, sin)
            m = _attention(q.reshape(B, S, -1), kn.reshape(B, S, -1), kr.reshape(B, S, -1),
                           v.reshape(B, S, -1)).reshape(T, D)
            w_o, b_o = mla_w_o[j], zero_bias
        elif kind == 1:
            q, k, v, lf, gate = _hg_proj(x2, g_mix, hg_w_in[j].astype(BF16), lb_table[layer][None, :])
            r3 = lambda t: t.reshape(B, S, -1)
            m = _hg_scan(r3(q), r3(k), r3(v), r3(lf), r3(gate), hg_out_norm[j][None, :]).reshape(T, D)
            w_o, b_o = hg_w_o[j], zero_bias
        else:
            u = _cv_glu(x2, g_mix, cv_w_pw1[j].astype(BF16), cv_b_pw1[j][None, :])
            m = _cv_dw(u.reshape(B, S, D), cv_w_dw[j], cv_b_dw[j][None, :], cv_ln_g[j][None, :],
                       cv_ln_b[j][None, :]).reshape(T, D)
            w_o, b_o = cv_w_pw2[j], cv_b_pw2[j][None, :]
        x2 = _tail(x2, m, w_o.astype(BF16), b_o, norm_mlp[layer][None, :],
                   mlp_w_in[layer].astype(BF16), mlp_w_out[layer].astype(BF16))
    return x2.reshape(B, S, D)
```

```python
import functools

import numpy as np
import jax
import jax.numpy as jnp
from jax import lax
from jax.experimental import pallas as pl
from jax.experimental.pallas import tpu as pltpu

F32 = jnp.float32
BF16 = jnp.bfloat16

D_MODEL = 1024
N_MIXERS = 3
MLA_HEADS = 8
MLA_Q_LORA = 384
MLA_KV_LORA = 256
MLA_NOPE = 128
MLA_ROPE = 64
MLA_V = 128
MLA_QK = MLA_NOPE + MLA_ROPE
ROPE_THETA = 10000.0
HG_HEADS = 8
HG_D = D_MODEL // HG_HEADS
HG_CHUNK = 64
CONV_WIDTH = 31
D_FF = 4 * D_MODEL
EPS = 1e-6
LOG2_E = 1.4426950408889634

LANES = 128
HEAD_PAD = 2 * LANES
VMEM_LIMIT = 56 * 1024 * 1024

TM_PROJ = 1024
TM_TAIL = 1024
TF_TAIL = 512
TQ_ATTN = 512
ATTN_ONES_ROWS = 16
TM_CONV = 512
CONV_HALO = 32
CONV_ROWS = 64
CONV_TAP_GROUP = 16
SUBLANES = 8


def _cparams(*sem):
    return pltpu.CompilerParams(dimension_semantics=sem, vmem_limit_bytes=VMEM_LIMIT)


def _rms(x, g):
    return x * lax.rsqrt(jnp.mean(x * x, axis=-1, keepdims=True) + EPS) * g


def _dot(a, b):
    return jnp.dot(a, b, preferred_element_type=F32)


def _dot_nt(a, b):
    return lax.dot_general(a, b, (((1,), (1,)), ((), ())), preferred_element_type=F32)


def _row_spec(tm, n):
    return pl.BlockSpec((tm, n), lambda i: (i, 0))


def _const_spec(shape):
    return pl.BlockSpec(shape, lambda *_: (0,) * len(shape))


def _resident_spec(shape):
    return pl.BlockSpec(shape, lambda *_: (0,) * len(shape), pipeline_mode=pl.Buffered(1))


def _rope_table_kernel(pos_ref, invf_ref, cos_ref, sin_ref):
    ang = pos_ref[...] * invf_ref[...]
    lane = lax.broadcasted_iota(jnp.int32, (1, LANES), 1)
    sign = jnp.where((lane // (MLA_ROPE // 2)) % 2 == 0, -1.0, 1.0)
    cos_ref[...] = jnp.cos(ang)
    sin_ref[...] = jnp.sin(ang) * sign


def _rope_tables(pos_col, invf4):
    T = pos_col.shape[0]
    tm = TM_PROJ
    return pl.pallas_call(
        _rope_table_kernel,
        grid=(T // tm,),
        in_specs=[_row_spec(tm, 1), _const_spec((1, LANES))],
        out_specs=[_row_spec(tm, LANES), _row_spec(tm, LANES)],
        out_shape=[jax.ShapeDtypeStruct((T, LANES), F32)] * 2,
        compiler_params=_cparams("parallel"),
        name="rope_tables",
    )(pos_col, invf4)


def _mla_proj_kernel(x_ref, g_ref, wd_ref, qlg_ref, kvlg_ref, wuq_ref, wukv_ref,
                     qng_ref, qrg_ref, kng_ref, krg_ref, cos_ref, sin_ref, ones_ref,
                     q_ref, kn_ref, kr_ref, v_ref):
    h = _rms(x_ref[...], g_ref[...]).astype(BF16)
    lat = _dot(h, wd_ref[...])
    c_q = _rms(lat[:, :MLA_Q_LORA], qlg_ref[...]).astype(BF16)
    c_kv = _rms(lat[:, MLA_Q_LORA:MLA_Q_LORA + MLA_KV_LORA], kvlg_ref[...]).astype(BF16)
    cos = cos_ref[...]
    sin = sin_ref[...]

    def rope(r):
        return r * cos + pltpu.roll(r, MLA_ROPE // 2, 1) * sin

    ones2 = ones_ref[...]
    ones1 = ones_ref[:LANES, :LANES]

    def group_rms(t, g, ones):
        return t * lax.rsqrt(_dot((t * t).astype(BF16), ones) + EPS) * g

    lane = lax.broadcasted_iota(jnp.int32, (1, LANES), 1)
    kr = rope(group_rms(lat[:, MLA_Q_LORA + MLA_KV_LORA:], krg_ref[...], ones1))
    kr_ref[...] = jnp.where(lane < MLA_ROPE, kr, 0.0).astype(BF16)

    scale = MLA_QK ** -0.5 * LOG2_E
    qg = jnp.concatenate([qng_ref[...], qrg_ref[...]], axis=1) * scale
    kng2 = jnp.concatenate([kng_ref[...], kng_ref[...]], axis=1)
    heads = range(MLA_HEADS)
    q = [_dot(c_q, wuq_ref[:, hh * HEAD_PAD:(hh + 1) * HEAD_PAD]) for hh in heads]
    kv = [_dot(c_kv, wukv_ref[:, hh * 2 * LANES:(hh + 1) * 2 * LANES]) for hh in heads]
    for hh in heads:
        v_ref[:, hh * LANES:(hh + 1) * LANES] = kv[hh][:, LANES:].astype(BF16)
    q = [group_rms(q[hh], qg, ones2) for hh in heads]
    for hh in range(0, MLA_HEADS, 2):
        kn = jnp.concatenate([kv[hh][:, :LANES], kv[hh + 1][:, :LANES]], axis=1)
        kn_ref[:, hh * LANES:(hh + 2) * LANES] = group_rms(kn, kng2, ones2).astype(BF16)
    for hh in heads:
        q_ref[:, hh * HEAD_PAD:hh * HEAD_PAD + LANES] = q[hh][:, :LANES].astype(BF16)
        q_ref[:, hh * HEAD_PAD + LANES:(hh + 1) * HEAD_PAD] = rope(q[hh][:, LANES:]).astype(BF16)


def _mla_proj(x2, g, wd, qlg, kvlg, wuq, wukv, qng, qrg, kng, krg, cos, sin):
    T = x2.shape[0]
    tm = TM_PROJ
    group = np.arange(HEAD_PAD) // LANES
    ones = jnp.asarray((group[:, None] == group[None, :]) / LANES, BF16)
    return pl.pallas_call(
        _mla_proj_kernel,
        grid=(T // tm,),
        in_specs=[_row_spec(tm, D_MODEL), _const_spec((1, D_MODEL)), _resident_spec(wd.shape),
                  _const_spec((1, MLA_Q_LORA)), _const_spec((1, MLA_KV_LORA)),
                  _resident_spec(wuq.shape), _resident_spec(wukv.shape),
                  _const_spec((1, LANES)), _const_spec((1, LANES)), _const_spec((1, LANES)),
                  _const_spec((1, LANES)), _row_spec(tm, LANES), _row_spec(tm, LANES),
                  _const_spec((HEAD_PAD, HEAD_PAD))],
        out_specs=[_row_spec(tm, MLA_HEADS * HEAD_PAD), _row_spec(tm, MLA_HEADS * LANES),
                   _row_spec(tm, LANES), _row_spec(tm, MLA_HEADS * LANES)],
        out_shape=[jax.ShapeDtypeStruct((T, MLA_HEADS * HEAD_PAD), BF16),
                   jax.ShapeDtypeStruct((T, MLA_HEADS * LANES), BF16),
                   jax.ShapeDtypeStruct((T, LANES), BF16),
                   jax.ShapeDtypeStruct((T, MLA_HEADS * LANES), BF16)],
        compiler_params=_cparams("parallel"),
        name="mla_proj",
    )(x2, g, wd, qlg, kvlg, wuq, wukv, qng, qrg, kng, krg, cos, sin, ones)


def _attn_kernel(q_ref, kn_ref, kr_ref, v_ref, o_ref, kcat_ref, vt_ref, *, tq):
    n_q = q_ref.shape[1] // tq
    kcat_ref[:, :LANES] = kn_ref[0]
    kcat_ref[:, LANES:] = kr_ref[0]
    vt_ref[:MLA_V, :] = v_ref[0].astype(F32).T.astype(BF16)
    vt_ref[MLA_V:, :] = jnp.ones((ATTN_ONES_ROWS, vt_ref.shape[1]), BF16)

    def scores(i, j):
        return _dot_nt(kcat_ref[j * tq:(j + 1) * tq, :], q_ref[0, i * tq:(i + 1) * tq, :])

    def accumulate(j, s, carry, masked):
        m, acc = carry
        if masked:
            key = lax.broadcasted_iota(jnp.int32, (tq, tq), 0)
            qry = lax.broadcasted_iota(jnp.int32, (tq, tq), 1)
            s = jnp.where(key <= qry, s, -jnp.inf)
        m_new = jnp.maximum(m, jnp.max(s, axis=0, keepdims=True))
        p = jnp.exp2(s - m_new)
        acc = jnp.exp2(m - m_new) * acc + _dot(vt_ref[:, j * tq:(j + 1) * tq], p.astype(BF16))
        return m_new, acc

    pairs = [(i, j) for i in range(n_q) for j in range(i + 1)]
    s_next = scores(*pairs[0])
    carry = None
    for n, (i, j) in enumerate(pairs):
        s, s_next = s_next, (scores(*pairs[n + 1]) if n + 1 < len(pairs) else None)
        if j == 0:
            carry = (jnp.full((1, tq), -jnp.inf, F32), jnp.zeros((MLA_V + ATTN_ONES_ROWS, tq), F32))
        carry = accumulate(j, s, carry, masked=(j == i))
        if j == i:
            acc = carry[1]
            o_ref[0, i * tq:(i + 1) * tq, :] = (acc[:MLA_V] / acc[MLA_V:MLA_V + 1]).T.astype(BF16)


def _attention(q, kn, kr, v):
    B, S, _ = q.shape
    return pl.pallas_call(
        functools.partial(_attn_kernel, tq=TQ_ATTN),
        grid=(B, MLA_HEADS),
        in_specs=[pl.BlockSpec((1, S, HEAD_PAD), lambda b, h: (b, 0, h)),
                  pl.BlockSpec((1, S, LANES), lambda b, h: (b, 0, h)),
                  pl.BlockSpec((1, S, LANES), lambda b, h: (b, 0, 0)),
                  pl.BlockSpec((1, S, LANES), lambda b, h: (b, 0, h))],
        out_specs=pl.BlockSpec((1, S, MLA_V), lambda b, h: (b, 0, h)),
        out_shape=jax.ShapeDtypeStruct((B, S, MLA_HEADS * MLA_V), BF16),
        scratch_shapes=[pltpu.VMEM((S, HEAD_PAD), BF16), pltpu.VMEM((MLA_V + ATTN_ONES_ROWS, S), BF16)],
        compiler_params=_cparams("parallel", "parallel"),
        name="mla_attention",
    )(q, kn, kr, v)


def _tail_kernel(x_ref, m_ref, wo_ref, bo_ref, g_ref, win_ref, wout_ref, o_ref):
    x1 = x_ref[...] + _dot(m_ref[...], wo_ref[...]) + bo_ref[...]
    h = _rms(x1, g_ref[...]).astype(BF16)
    acc = x1
    for kk in range(D_FF // TF_TAIL):
        a = jnp.maximum(_dot(h, win_ref[:, kk * TF_TAIL:(kk + 1) * TF_TAIL]), 0.0)
        acc = acc + _dot((a * a).astype(BF16), wout_ref[kk * TF_TAIL:(kk + 1) * TF_TAIL, :])
    o_ref[...] = acc


def _tail(x2, m2, wo, bo, g, win, wout):
    T = x2.shape[0]
    tm = TM_TAIL
    return pl.pallas_call(
        _tail_kernel,
        grid=(T // tm,),
        in_specs=[_row_spec(tm, D_MODEL), _row_spec(tm, D_MODEL), _resident_spec(wo.shape),
                  _const_spec((1, D_MODEL)), _const_spec((1, D_MODEL)),
                  _resident_spec(win.shape), _resident_spec(wout.shape)],
        out_specs=_row_spec(tm, D_MODEL),
        out_shape=jax.ShapeDtypeStruct((T, D_MODEL), F32),
        compiler_params=_cparams("parallel"),
        name="tail_mlp",
    )(x2, m2, wo, bo, g, win, wout)


def _hg_lb_kernel(logits_ref, lb_ref):
    z = logits_ref[...]
    e = jnp.exp(z - jnp.max(z, axis=0, keepdims=True))
    sm = e / jnp.sum(e, axis=0, keepdims=True)
    depth = z.shape[0]
    acc = sm[0:1]
    first = acc
    lb_ref[0:1, :] = acc - first
    for i in range(1, depth):
        acc = acc + sm[i:i + 1]
        lb_ref[i:i + 1, :] = acc - first


def _hg_lb_table(logits):
    return pl.pallas_call(
        _hg_lb_kernel,
        out_shape=jax.ShapeDtypeStruct(logits.shape, F32),
        name="hgrn_lower_bounds",
    )(logits)


def _hg_proj_kernel(x_ref, g_ref, w_ref, lb_ref, q_ref, k_ref, v_ref, lf_ref, gate_ref):
    h = _rms(x_ref[...], g_ref[...]).astype(BF16)
    lb = lb_ref[...]
    n = HG_HEADS * HG_D
    fz = _dot(h, w_ref[:, n:2 * n])
    gz = _dot(h, w_ref[:, 3 * n:4 * n])
    q_ref[...] = _dot(h, w_ref[:, 0:n]).astype(BF16)
    lf_ref[...] = jnp.log2(lb + (1.0 - lb) * jax.nn.sigmoid(fz))
    k_ref[...] = ((1.0 - lb) * jax.nn.sigmoid(-fz)).astype(BF16)
    v_ref[...] = _dot(h, w_ref[:, 2 * n:3 * n]).astype(BF16)
    gate_ref[...] = (gz * jax.nn.sigmoid(gz)).astype(BF16)


def _hg_proj(x2, g, w, lb):
    T = x2.shape[0]
    tm = TM_PROJ
    n = HG_HEADS * HG_D
    bf = jax.ShapeDtypeStruct((T, n), BF16)
    return pl.pallas_call(
        _hg_proj_kernel,
        grid=(T // tm,),
        in_specs=[_row_spec(tm, D_MODEL), _const_spec((1, D_MODEL)), _resident_spec(w.shape),
                  _const_spec((1, n))],
        out_specs=[_row_spec(tm, n)] * 5,
        out_shape=[bf, bf, bf, jax.ShapeDtypeStruct((T, n), F32), bf],
        compiler_params=_cparams("parallel"),
        name="hgrn_proj",
    )(x2, g, w, lb)


HG_LEVELS = 6
HG_COARSE = 3
HG_GROUP = 32
HG_OUT_GROUP = 32


def _hg_level(lvl):
    C = HG_CHUNK
    t = np.arange(C)
    half = (C // 2) >> lvl
    blk = t // (2 * half)
    second = (t % (2 * half)) >= half
    mid = blk * 2 * half + half - 1
    return half, blk, second, mid


def _hg_constants():
    C = HG_CHUNK
    t = np.arange(C)
    j = t[None, :]
    mats = [j <= t[:, None]]
    signs, masks = [], []
    for lvl in range(HG_LEVELS):
        half, blk, second, mid = _hg_level(lvl)
        if lvl < HG_COARSE:
            signs.append(np.where(second, 1.0, -1.0)[:, None] * np.ones((1, HG_D)))
        else:
            a_q = (j > mid[:, None]) & (j <= t[:, None])
            a_k = (j > t[:, None]) & (j <= mid[:, None])
            mats.append(np.where(second[:, None], a_q, a_k))
        masks.append((blk[:, None] == blk[None, :]) & second[:, None] & (~second)[None, :])
    masks.append(np.eye(C, dtype=bool))
    a = np.concatenate(mats, axis=0).astype(np.float32)
    a3 = np.concatenate([a, a, a], axis=1)
    return (jnp.asarray(a3, BF16), jnp.asarray(np.stack(signs), F32),
            jnp.asarray(np.stack(masks).astype(np.float32)))


def _hg_scan_kernel(q_ref, k_ref, v_ref, lf_ref, gate_ref, a3_ref, sign_ref, mask_ref, gout_ref, o_ref,
                    qh_ref, oi_ref, st_ref):
    C = HG_CHUNK
    n_chunks = q_ref.shape[1] // C
    gout = gout_ref[...]

    G = min(HG_GROUP, n_chunks)
    assert n_chunks % G == 0
    grp = range(G)

    def group(gi, state):
        rows = [pl.ds(pl.multiple_of((gi * G + c) * C, C), C) for c in grp]
        lf3 = []
        for c in grp:
            lf = lf_ref[0, rows[c], :]
            hi = lf.astype(BF16)
            r1 = lf - hi.astype(F32)
            mid = r1.astype(BF16)
            lo = (r1 - mid.astype(F32)).astype(BF16)
            lf3.append(jnp.concatenate([hi, mid, lo], axis=0))
        seg = _dot(a3_ref[...], jnp.concatenate(lf3, axis=1))
        seg = [seg[:, c * HG_D:(c + 1) * HG_D] for c in grp]
        b = [seg[c][0:C] for c in grp]
        q = [q_ref[0, rows[c], :].astype(F32) for c in grp]
        k = [k_ref[0, rows[c], :].astype(F32) for c in grp]
        v = [v_ref[0, rows[c], :] for c in grp]
        attn = [_dot_nt(q[c].astype(BF16), k[c].astype(BF16)) * mask_ref[HG_LEVELS] for c in grp]
        upd = [lax.dot_general(v[c], (k[c] * jnp.exp2(b[c][C - 1:C] - b[c])).astype(BF16),
                               (((0,), (0,)), ((), ())), preferred_element_type=F32) for c in grp]
        for c in grp:
            qh_ref[rows[c], :] = (q[c] * jnp.exp2(b[c])).astype(BF16)
        for lvl in range(HG_LEVELS):
            for c in grp:
                if lvl < HG_COARSE:
                    half, _, _, mids = _hg_level(lvl)
                    b_mid = jnp.concatenate([jnp.broadcast_to(b[c][m:m + 1], (2 * half, HG_D))
                                             for m in mids[::2 * half]], axis=0)
                    e = (b[c] - b_mid) * sign_ref[lvl]
                else:
                    e = seg[c][(1 + lvl - HG_COARSE) * C:(2 + lvl - HG_COARSE) * C]
                xl = jnp.exp2(e)
                attn[c] = attn[c] + _dot_nt((q[c] * xl).astype(BF16), (k[c] * xl).astype(BF16)) * mask_ref[lvl]
        for c in grp:
            oi_ref[rows[c], :] = _dot(attn[c].astype(BF16), v[c])
            st_ref[gi * G + c] = state.astype(BF16)
            state = state * jnp.exp2(b[c][C - 1:C]) + upd[c]
        return state

    lax.fori_loop(0, n_chunks // G, group, jnp.zeros((HG_D, HG_D), F32))

    GO = min(HG_OUT_GROUP, n_chunks)
    assert n_chunks % GO == 0
    out_grp = range(GO)

    def out_group(gi, carry):
        rows = [pl.ds(pl.multiple_of((gi * GO + c) * C, C), C) for c in out_grp]
        o = [_dot_nt(qh_ref[rows[c], :], st_ref[gi * GO + c]) + oi_ref[rows[c], :] for c in out_grp]
        o = [o[c] * lax.rsqrt(jnp.mean(o[c] * o[c], axis=-1, keepdims=True) + EPS) * gout for c in out_grp]
        for c in out_grp:
            o_ref[0, rows[c], :] = (o[c] * gate_ref[0, rows[c], :].astype(F32)).astype(BF16)
        return carry

    lax.fori_loop(0, n_chunks // GO, out_group, 0)


def _hg_scan(q, k, v, lf, gate, gout):
    B, S, _ = q.shape
    a3, signs, masks = _hg_constants()
    blk = pl.BlockSpec((1, S, HG_D), lambda b, h: (b, 0, h))
    return pl.pallas_call(
        _hg_scan_kernel,
        grid=(B, HG_HEADS),
        in_specs=[blk, blk, blk, blk, blk, _const_spec(a3.shape), _const_spec(signs.shape),
                  _const_spec(masks.shape), _const_spec((1, HG_D))],
        out_specs=blk,
        out_shape=jax.ShapeDtypeStruct((B, S, HG_HEADS * HG_D), BF16),
        scratch_shapes=[pltpu.VMEM((S, HG_D), BF16),
                        pltpu.VMEM((S, HG_D), F32),
                        pltpu.VMEM((S // HG_CHUNK, HG_D, HG_D), BF16)],
        compiler_params=_cparams("parallel", "parallel"),
        name="hgrn_scan",
    )(q, k, v, lf, gate, a3, signs, masks, gout)


def _cv_glu_kernel(x_ref, g_ref, w_ref, b_ref, u_ref):
    h = _rms(x_ref[...], g_ref[...]).astype(BF16)
    a = _dot(h, w_ref[:, :D_MODEL]) + b_ref[:, :D_MODEL]
    gate = _dot(h, w_ref[:, D_MODEL:]) + b_ref[:, D_MODEL:]
    u_ref[...] = a * jax.nn.sigmoid(gate)


def _cv_glu(x2, g, w, b):
    T = x2.shape[0]
    tm = TM_PROJ
    return pl.pallas_call(
        _cv_glu_kernel,
        grid=(T // tm,),
        in_specs=[_row_spec(tm, D_MODEL), _const_spec((1, D_MODEL)), _resident_spec(w.shape),
                  _const_spec((1, 2 * D_MODEL))],
        out_specs=_row_spec(tm, D_MODEL),
        out_shape=jax.ShapeDtypeStruct((T, D_MODEL), F32),
        compiler_params=_cparams("parallel"),
        name="conv_glu",
    )(x2, g, w, b)


def _cv_dw_kernel(u_ref, prev_ref, w_ref, b_ref, lng_ref, lnb_ref, o_ref, buf_ref, sh_ref, acc_ref, *, tm):
    i = pl.program_id(1)
    halo = prev_ref[0]
    buf_ref[0:CONV_HALO, :] = jnp.where(i == 0, 0.0, halo)
    buf_ref[CONV_HALO:, :] = u_ref[0]
    n_sh = sh_ref.shape[1]
    for r in range(1, SUBLANES):
        sh_ref[r - 1] = buf_ref[r:r + n_sh, :]
    off = CONV_HALO - (CONV_WIDTH - 1)
    rb = CONV_ROWS
    for ct in range(D_MODEL // LANES):
        cols = slice(ct * LANES, (ct + 1) * LANES)
        for j0 in range(0, CONV_WIDTH, CONV_TAP_GROUP):
            group = range(j0, min(j0 + CONV_TAP_GROUP, CONV_WIDTH))
            taps = {j: w_ref[j:j + 1, cols] for j in group}
            for base in range(0, tm, rb):
                acc = jnp.zeros((rb, LANES), F32) + b_ref[:, cols] if j0 == 0 else acc_ref[base:base + rb, cols]
                for j in group:
                    a, r = divmod(off + j, SUBLANES)
                    lo = base + a * SUBLANES
                    src = buf_ref[lo:lo + rb, cols] if r == 0 else sh_ref[r - 1, lo:lo + rb, cols]
                    acc = acc + src * taps[j]
                acc_ref[base:base + rb, cols] = acc
    acc = acc_ref[...]
    mu = jnp.mean(acc, axis=-1, keepdims=True)
    xc = acc - mu
    y = xc * lax.rsqrt(jnp.mean(xc * xc, axis=-1, keepdims=True) + EPS) * lng_ref[...] + lnb_ref[...]
    o_ref[0] = (y * jax.nn.sigmoid(y)).astype(BF16)


def _cv_dw(u, w, b, lng, lnb):
    B, S, _ = u.shape
    tm = TM_CONV
    ratio = tm // CONV_HALO
    return pl.pallas_call(
        functools.partial(_cv_dw_kernel, tm=tm),
        grid=(B, S // tm),
        in_specs=[pl.BlockSpec((1, tm, D_MODEL), lambda b, i: (b, i, 0)),
                  pl.BlockSpec((1, CONV_HALO, D_MODEL), lambda b, i: (b, jnp.maximum(i * ratio - 1, 0), 0)),
                  _const_spec((CONV_WIDTH, D_MODEL)), _const_spec((1, D_MODEL)),
                  _const_spec((1, D_MODEL)), _const_spec((1, D_MODEL))],
        out_specs=pl.BlockSpec((1, tm, D_MODEL), lambda b, i: (b, i, 0)),
        out_shape=jax.ShapeDtypeStruct((B, S, D_MODEL), BF16),
        scratch_shapes=[pltpu.VMEM((tm + CONV_HALO, D_MODEL), F32),
                        pltpu.VMEM((SUBLANES - 1, tm + CONV_HALO - SUBLANES, D_MODEL), F32),
                        pltpu.VMEM((tm, D_MODEL), F32)],
        compiler_params=_cparams("parallel", "parallel"),
        name="conv_depthwise",
    )(u, u, w, b, lng, lnb)


def _dup_rope(t):
    return jnp.concatenate([t, t], axis=-1)


def _mla_weights(w_down, w_uq, q_head_norm, k_head_norm):
    wd = jnp.concatenate([w_down, w_down[:, -MLA_ROPE:]], axis=1).astype(BF16)
    wq = w_uq.reshape(MLA_Q_LORA, MLA_HEADS, MLA_QK)
    wq = jnp.concatenate([wq, wq[:, :, MLA_NOPE:]], axis=-1)
    wq = wq.reshape(MLA_Q_LORA, MLA_HEADS * HEAD_PAD).astype(BF16)
    qng = q_head_norm[None, :MLA_NOPE]
    qrg = _dup_rope(q_head_norm[None, MLA_NOPE:])
    kng = k_head_norm[None, :MLA_NOPE]
    krg = _dup_rope(k_head_norm[None, MLA_NOPE:])
    return wd, wq, qng, qrg, kng, krg


def kernel(x, positions, norm_mix, norm_mlp, mlp_w_in, mlp_w_out, mla_w_down, mla_q_lat_norm, mla_kv_lat_norm, mla_w_uq, mla_w_ukv, mla_q_head_norm, mla_k_head_norm, mla_w_o, hg_w_in, hg_lb_logits, hg_out_norm, hg_w_o, cv_w_pw1, cv_b_pw1, cv_w_dw, cv_b_dw, cv_ln_g, cv_ln_b, cv_w_pw2, cv_b_pw2):
    B, S, D = x.shape
    T = B * S
    depth = norm_mix.shape[0]
    x2 = x.reshape(T, D)

    inv_freq = jnp.power(ROPE_THETA, -jnp.arange(0, MLA_ROPE, 2, dtype=F32) / MLA_ROPE)
    invf4 = jnp.tile(inv_freq, 4)[None, :]
    cos, sin = _rope_tables(positions.astype(F32).reshape(T, 1), invf4)
    lb_table = _hg_lb_table(hg_lb_logits.astype(F32))
    zero_bias = jnp.zeros((1, D), F32)

    for layer in range(depth):
        kind, j = layer % N_MIXERS, layer // N_MIXERS
        g_mix = norm_mix[layer][None, :]
        if kind == 0:
            wd, wq, qng, qrg, kng, krg = _mla_weights(mla_w_down[j], mla_w_uq[j], mla_q_head_norm[j],
                                                      mla_k_head_norm[j])
            q, kn, kr, v = _mla_proj(x2, g_mix, wd, mla_q_lat_norm[j][None, :], mla_kv_lat_norm[j][None, :],
                                     wq, mla_w_ukv[j].astype(BF16), qng, qrg, kng, krg, cos, sin)
            m = _attention(q.reshape(B, S, -1), kn.reshape(B, S, -1), kr.reshape(B, S, -1),
                           v.reshape(B, S, -1)).reshape(T, D)
            w_o, b_o = mla_w_o[j], zero_bias
        elif kind == 1:
            q, k, v, lf, gate = _hg_proj(x2, g_mix, hg_w_in[j].astype(BF16), lb_table[layer][None, :])
            r3 = lambda t: t.reshape(B, S, -1)
            m = _hg_scan(r3(q), r3(k), r3(v), r3(lf), r3(gate), hg_out_norm[j][None, :]).reshape(T, D)
            w_o, b_o = hg_w_o[j], zero_bias
        else:
            u = _cv_glu(x2, g_mix, cv_w_pw1[j].astype(BF16), cv_b_pw1[j][None, :])
            m = _cv_dw(u.reshape(B, S, D), cv_w_dw[j], cv_b_dw[j][None, :], cv_ln_g[j][None, :],
                       cv_ln_b[j][None, :]).reshape(T, D)
            w_o, b_o = cv_w_pw2[j], cv_b_pw2[j][None, :]
        x2 = _tail(x2, m, w_o.astype(BF16), b_o, norm_mlp[layer][None, :],
                   mlp_w_in[layer].astype(BF16), mlp_w_out[layer].astype(BF16))
    return x2.reshape(B, S, D)
```

```python
import functools

import numpy as np
import jax
import jax.numpy as jnp
from jax import lax
from jax.experimental import pallas as pl
from jax.experimental.pallas import tpu as pltpu

F32 = jnp.float32
BF16 = jnp.bfloat16

D_MODEL = 1024
N_MIXERS = 3
MLA_HEADS = 8
MLA_Q_LORA = 384
MLA_KV_LORA = 256
MLA_NOPE = 128
MLA_ROPE = 64
MLA_V = 128
MLA_QK = MLA_NOPE + MLA_ROPE
ROPE_THETA = 10000.0
HG_HEADS = 8
HG_D = D_MODEL // HG_HEADS
HG_CHUNK = 64
CONV_WIDTH = 31
D_FF = 4 * D_MODEL
EPS = 1e-6
LOG2_E = 1.4426950408889634

LANES = 128
HEAD_PAD = 2 * LANES
VMEM_LIMIT = 56 * 1024 * 1024

TM_PROJ = 1024
TM_TAIL = 1024
TF_TAIL = 512
TQ_ATTN = 512
ATTN_ONES_ROWS = 16
TM_CONV = 512
CONV_HALO = 32
CONV_ROWS = 64
CONV_TAP_GROUP = 16
SUBLANES = 8


def _cparams(*sem):
    return pltpu.CompilerParams(dimension_semantics=sem, vmem_limit_bytes=VMEM_LIMIT)


def _rms(x, g):
    return x * lax.rsqrt(jnp.mean(x * x, axis=-1, keepdims=True) + EPS) * g


def _dot(a, b):
    return jnp.dot(a, b, preferred_element_type=F32)


def _dot_nt(a, b):
    return lax.dot_general(a, b, (((1,), (1,)), ((), ())), preferred_element_type=F32)


def _row_spec(tm, n):
    return pl.BlockSpec((tm, n), lambda i: (i, 0))


def _const_spec(shape):
    return pl.BlockSpec(shape, lambda *_: (0,) * len(shape))


def _resident_spec(shape):
    return pl.BlockSpec(shape, lambda *_: (0,) * len(shape), pipeline_mode=pl.Buffered(1))


def _rope_table_kernel(pos_ref, invf_ref, cos_ref, sin_ref):
    ang = pos_ref[...] * invf_ref[...]
    lane = lax.broadcasted_iota(jnp.int32, (1, LANES), 1)
    sign = jnp.where((lane // (MLA_ROPE // 2)) % 2 == 0, -1.0, 1.0)
    cos_ref[...] = jnp.cos(ang)
    sin_ref[...] = jnp.sin(ang) * sign


def _rope_tables(pos_col, invf4):
    T = pos_col.shape[0]
    tm = TM_PROJ
    return pl.pallas_call(
        _rope_table_kernel,
        grid=(T // tm,),
        in_specs=[_row_spec(tm, 1), _const_spec((1, LANES))],
        out_specs=[_row_spec(tm, LANES), _row_spec(tm, LANES)],
        out_shape=[jax.ShapeDtypeStruct((T, LANES), F32)] * 2,
        compiler_params=_cparams("parallel"),
        name="rope_tables",
    )(pos_col, invf4)


def _mla_proj_kernel(x_ref, g_ref, wd_ref, qlg_ref, kvlg_ref, wuq_ref, wukv_ref,
                     qng_ref, qrg_ref, kng_ref, krg_ref, cos_ref, sin_ref, ones_ref,
                     q_ref, kn_ref, kr_ref, v_ref):
    h = _rms(x_ref[...], g_ref[...]).astype(BF16)
    lat = _dot(h, wd_ref[...])
    c_q = _rms(lat[:, :MLA_Q_LORA], qlg_ref[...]).astype(BF16)
    c_kv = _rms(lat[:, MLA_Q_LORA:MLA_Q_LORA + MLA_KV_LORA], kvlg_ref[...]).astype(BF16)
    cos = cos_ref[...]
    sin = sin_ref[...]

    def rope(r):
        return r * cos + pltpu.roll(r, MLA_ROPE // 2, 1) * sin

    ones2 = ones_ref[...]
    ones1 = ones_ref[:LANES, :LANES]

    def group_rms(t, g, ones):
        return t * lax.rsqrt(_dot((t * t).astype(BF16), ones) + EPS) * g

    lane = lax.broadcasted_iota(jnp.int32, (1, LANES), 1)
    kr = rope(group_rms(lat[:, MLA_Q_LORA + MLA_KV_LORA:], krg_ref[...], ones1))
    kr_ref[...] = jnp.where(lane < MLA_ROPE, kr, 0.0).astype(BF16)

    scale = MLA_QK ** -0.5 * LOG2_E
    qg = jnp.concatenate([qng_ref[...], qrg_ref[...]], axis=1) * scale
    kng2 = jnp.concatenate([kng_ref[...], kng_ref[...]], axis=1)
    heads = range(MLA_HEADS)
    q = [_dot(c_q, wuq_ref[:, hh * HEAD_PAD:(hh + 1) * HEAD_PAD]) for hh in heads]
    kv = [_dot(c_kv, wukv_ref[:, hh * 2 * LANES:(hh + 1) * 2 * LANES]) for hh in heads]
    for hh in heads:
        v_ref[0, hh] = kv[hh][:, LANES:].astype(BF16)
    q = [group_rms(q[hh], qg, ones2) for hh in heads]
    for hh in range(0, MLA_HEADS, 2):
        kn = jnp.concatenate([kv[hh][:, :LANES], kv[hh + 1][:, :LANES]], axis=1)
        kn = group_rms(kn, kng2, ones2).astype(BF16)
        kn_ref[0, hh] = kn[:, :LANES]
        kn_ref[0, hh + 1] = kn[:, LANES:]
    for hh in heads:
        q_ref[0, hh, :, :LANES] = q[hh][:, :LANES].astype(BF16)
        q_ref[0, hh, :, LANES:] = rope(q[hh][:, LANES:]).astype(BF16)


def _mla_proj(x2, seq_len, g, wd, qlg, kvlg, wuq, wukv, qng, qrg, kng, krg, cos, sin):
    T = x2.shape[0]
    tm = min(TM_PROJ, seq_len)
    n_s = seq_len // tm
    B = T // seq_len

    def head_major(width):
        return pl.BlockSpec((1, MLA_HEADS, tm, width), lambda i: (i // n_s, 0, i % n_s, 0))

    group = np.arange(HEAD_PAD) // LANES
    ones = jnp.asarray((group[:, None] == group[None, :]) / LANES, BF16)
    return pl.pallas_call(
        _mla_proj_kernel,
        grid=(T // tm,),
        in_specs=[_row_spec(tm, D_MODEL), _const_spec((1, D_MODEL)), _resident_spec(wd.shape),
                  _const_spec((1, MLA_Q_LORA)), _const_spec((1, MLA_KV_LORA)),
                  _resident_spec(wuq.shape), _resident_spec(wukv.shape),
                  _const_spec((1, LANES)), _const_spec((1, LANES)), _const_spec((1, LANES)),
                  _const_spec((1, LANES)), _row_spec(tm, LANES), _row_spec(tm, LANES),
                  _const_spec((HEAD_PAD, HEAD_PAD))],
        out_specs=[head_major(HEAD_PAD), head_major(LANES), _row_spec(tm, LANES), head_major(LANES)],
        out_shape=[jax.ShapeDtypeStruct((B, MLA_HEADS, seq_len, HEAD_PAD), BF16),
                   jax.ShapeDtypeStruct((B, MLA_HEADS, seq_len, LANES), BF16),
                   jax.ShapeDtypeStruct((T, LANES), BF16),
                   jax.ShapeDtypeStruct((B, MLA_HEADS, seq_len, LANES), BF16)],
        compiler_params=_cparams("parallel"),
        name="mla_proj",
    )(x2, g, wd, qlg, kvlg, wuq, wukv, qng, qrg, kng, krg, cos, sin, ones)


def _attn_kernel(q_ref, kn_ref, kr_ref, v_ref, o_ref, kcat_ref, vt_ref, *, tq):
    n_q = q_ref.shape[2] // tq
    kcat_ref[:, :LANES] = kn_ref[0, 0]
    kcat_ref[:, LANES:] = kr_ref[0]
    vt_ref[:MLA_V, :] = v_ref[0, 0].astype(F32).T.astype(BF16)
    vt_ref[MLA_V:, :] = jnp.ones((ATTN_ONES_ROWS, vt_ref.shape[1]), BF16)

    def scores(i, j):
        return _dot_nt(kcat_ref[j * tq:(j + 1) * tq, :], q_ref[0, 0, i * tq:(i + 1) * tq, :])

    def accumulate(j, s, carry, masked):
        m, acc = carry
        if masked:
            key = lax.broadcasted_iota(jnp.int32, (tq, tq), 0)
            qry = lax.broadcasted_iota(jnp.int32, (tq, tq), 1)
            s = jnp.where(key <= qry, s, -jnp.inf)
        m_new = jnp.maximum(m, jnp.max(s, axis=0, keepdims=True))
        p = jnp.exp2(s - m_new)
        acc = jnp.exp2(m - m_new) * acc + _dot(vt_ref[:, j * tq:(j + 1) * tq], p.astype(BF16))
        return m_new, acc

    pairs = [(i, j) for i in range(n_q) for j in range(i + 1)]
    s_next = scores(*pairs[0])
    carry = None
    for n, (i, j) in enumerate(pairs):
        s, s_next = s_next, (scores(*pairs[n + 1]) if n + 1 < len(pairs) else None)
        if j == 0:
            carry = (jnp.full((1, tq), -jnp.inf, F32), jnp.zeros((MLA_V + ATTN_ONES_ROWS, tq), F32))
        carry = accumulate(j, s, carry, masked=(j == i))
        if j == i:
            acc = carry[1]
            o_ref[0, i * tq:(i + 1) * tq, :] = (acc[:MLA_V] / acc[MLA_V:MLA_V + 1]).T.astype(BF16)


def _attention(q, kn, kr, v):
    B, _, S, _ = q.shape
    slab = lambda width: pl.BlockSpec((1, 1, S, width), lambda b, h: (b, h, 0, 0))
    return pl.pallas_call(
        functools.partial(_attn_kernel, tq=TQ_ATTN),
        grid=(B, MLA_HEADS),
        in_specs=[slab(HEAD_PAD), slab(LANES), pl.BlockSpec((1, S, LANES), lambda b, h: (b, 0, 0)), slab(LANES)],
        out_specs=pl.BlockSpec((1, S, MLA_V), lambda b, h: (b, 0, h)),
        out_shape=jax.ShapeDtypeStruct((B, S, MLA_HEADS * MLA_V), BF16),
        scratch_shapes=[pltpu.VMEM((S, HEAD_PAD), BF16), pltpu.VMEM((MLA_V + ATTN_ONES_ROWS, S), BF16)],
        compiler_params=_cparams("parallel", "parallel"),
        name="mla_attention",
    )(q, kn, kr, v)


def _tail_kernel(x_ref, m_ref, wo_ref, bo_ref, g_ref, win_ref, wout_ref, o_ref):
    x1 = x_ref[...] + _dot(m_ref[...], wo_ref[...]) + bo_ref[...]
    h = _rms(x1, g_ref[...]).astype(BF16)
    acc = x1
    for kk in range(D_FF // TF_TAIL):
        a = jnp.maximum(_dot(h, win_ref[:, kk * TF_TAIL:(kk + 1) * TF_TAIL]), 0.0)
        acc = acc + _dot((a * a).astype(BF16), wout_ref[kk * TF_TAIL:(kk + 1) * TF_TAIL, :])
    o_ref[...] = acc


def _tail(x2, m2, wo, bo, g, win, wout):
    T = x2.shape[0]
    tm = TM_TAIL
    return pl.pallas_call(
        _tail_kernel,
        grid=(T // tm,),
        in_specs=[_row_spec(tm, D_MODEL), _row_spec(tm, D_MODEL), _resident_spec(wo.shape),
                  _const_spec((1, D_MODEL)), _const_spec((1, D_MODEL)),
                  _resident_spec(win.shape), _resident_spec(wout.shape)],
        out_specs=_row_spec(tm, D_MODEL),
        out_shape=jax.ShapeDtypeStruct((T, D_MODEL), F32),
        compiler_params=_cparams("parallel"),
        name="tail_mlp",
    )(x2, m2, wo, bo, g, win, wout)


def _hg_lb_kernel(logits_ref, lb_ref):
    z = logits_ref[...]
    e = jnp.exp(z - jnp.max(z, axis=0, keepdims=True))
    sm = e / jnp.sum(e, axis=0, keepdims=True)
    depth = z.shape[0]
    acc = sm[0:1]
    first = acc
    lb_ref[0:1, :] = acc - first
    for i in range(1, depth):
        acc = acc + sm[i:i + 1]
        lb_ref[i:i + 1, :] = acc - first


def _hg_lb_table(logits):
    return pl.pallas_call(
        _hg_lb_kernel,
        out_shape=jax.ShapeDtypeStruct(logits.shape, F32),
        name="hgrn_lower_bounds",
    )(logits)


def _hg_proj_kernel(x_ref, g_ref, w_ref, lb_ref, q_ref, k_ref, v_ref, lf_ref, gate_ref):
    h = _rms(x_ref[...], g_ref[...]).astype(BF16)
    lb = lb_ref[...]
    n = HG_HEADS * HG_D
    fz = _dot(h, w_ref[:, n:2 * n])
    gz = _dot(h, w_ref[:, 3 * n:4 * n])
    q_ref[...] = _dot(h, w_ref[:, 0:n]).astype(BF16)
    lf_ref[...] = jnp.log2(lb + (1.0 - lb) * jax.nn.sigmoid(fz))
    k_ref[...] = ((1.0 - lb) * jax.nn.sigmoid(-fz)).astype(BF16)
    v_ref[...] = _dot(h, w_ref[:, 2 * n:3 * n]).astype(BF16)
    gate_ref[...] = (gz * jax.nn.sigmoid(gz)).astype(BF16)


def _hg_proj(x2, g, w, lb):
    T = x2.shape[0]
    tm = TM_PROJ
    n = HG_HEADS * HG_D
    bf = jax.ShapeDtypeStruct((T, n), BF16)
    return pl.pallas_call(
        _hg_proj_kernel,
        grid=(T // tm,),
        in_specs=[_row_spec(tm, D_MODEL), _const_spec((1, D_MODEL)), _resident_spec(w.shape),
                  _const_spec((1, n))],
        out_specs=[_row_spec(tm, n)] * 5,
        out_shape=[bf, bf, bf, jax.ShapeDtypeStruct((T, n), F32), bf],
        compiler_params=_cparams("parallel"),
        name="hgrn_proj",
    )(x2, g, w, lb)


HG_LEVELS = 6
HG_COARSE = 3
HG_GROUP = 32
HG_OUT_GROUP = 32


def _hg_level(lvl):
    C = HG_CHUNK
    t = np.arange(C)
    half = (C // 2) >> lvl
    blk = t // (2 * half)
    second = (t % (2 * half)) >= half
    mid = blk * 2 * half + half - 1
    return half, blk, second, mid


def _hg_constants():
    C = HG_CHUNK
    t = np.arange(C)
    j = t[None, :]
    mats = [j <= t[:, None]]
    signs, masks = [], []
    for lvl in range(HG_LEVELS):
        half, blk, second, mid = _hg_level(lvl)
        if lvl < HG_COARSE:
            signs.append(np.where(second, 1.0, -1.0)[:, None] * np.ones((1, HG_D)))
        else:
            a_q = (j > mid[:, None]) & (j <= t[:, None])
            a_k = (j > t[:, None]) & (j <= mid[:, None])
            mats.append(np.where(second[:, None], a_q, a_k))
        masks.append((blk[:, None] == blk[None, :]) & second[:, None] & (~second)[None, :])
    masks.append(np.eye(C, dtype=bool))
    a = np.concatenate(mats, axis=0).astype(np.float32)
    a3 = np.concatenate([a, a, a], axis=1)
    return (jnp.asarray(a3, BF16), jnp.asarray(np.stack(signs), F32),
            jnp.asarray(np.stack(masks).astype(np.float32)))


def _hg_scan_kernel(q_ref, k_ref, v_ref, lf_ref, gate_ref, a3_ref, sign_ref, mask_ref, gout_ref, o_ref,
                    qh_ref, oi_ref, st_ref):
    C = HG_CHUNK
    n_chunks = q_ref.shape[1] // C
    gout = gout_ref[...]

    G = min(HG_GROUP, n_chunks)
    assert n_chunks % G == 0
    grp = range(G)

    def group(gi, state):
        rows = [pl.ds(pl.multiple_of((gi * G + c) * C, C), C) for c in grp]
        lf3 = []
        for c in grp:
            lf = lf_ref[0, rows[c], :]
            hi = lf.astype(BF16)
            r1 = lf - hi.astype(F32)
            mid = r1.astype(BF16)
            lo = (r1 - mid.astype(F32)).astype(BF16)
            lf3.append(jnp.concatenate([hi, mid, lo], axis=0))
        seg = _dot(a3_ref[...], jnp.concatenate(lf3, axis=1))
        seg = [seg[:, c * HG_D:(c + 1) * HG_D] for c in grp]
        b = [seg[c][0:C] for c in grp]
        q = [q_ref[0, rows[c], :].astype(F32) for c in grp]
        k = [k_ref[0, rows[c], :].astype(F32) for c in grp]
        v = [v_ref[0, rows[c], :] for c in grp]
        attn = [_dot_nt(q[c].astype(BF16), k[c].astype(BF16)) * mask_ref[HG_LEVELS] for c in grp]
        upd = [lax.dot_general(v[c], (k[c] * jnp.exp2(b[c][C - 1:C] - b[c])).astype(BF16),
                               (((0,), (0,)), ((), ())), preferred_element_type=F32) for c in grp]
        for c in grp:
            qh_ref[rows[c], :] = (q[c] * jnp.exp2(b[c])).astype(BF16)
        for lvl in range(HG_LEVELS):
            for c in grp:
                if lvl < HG_COARSE:
                    half, _, _, mids = _hg_level(lvl)
                    b_mid = jnp.concatenate([jnp.broadcast_to(b[c][m:m + 1], (2 * half, HG_D))
                                             for m in mids[::2 * half]], axis=0)
                    e = (b[c] - b_mid) * sign_ref[lvl]
                else:
                    e = seg[c][(1 + lvl - HG_COARSE) * C:(2 + lvl - HG_COARSE) * C]
                xl = jnp.exp2(e)
                attn[c] = attn[c] + _dot_nt((q[c] * xl).astype(BF16), (k[c] * xl).astype(BF16)) * mask_ref[lvl]
        for c in grp:
            oi_ref[rows[c], :] = _dot(attn[c].astype(BF16), v[c])
            st_ref[gi * G + c] = state.astype(BF16)
            state = state * jnp.exp2(b[c][C - 1:C]) + upd[c]
        return state

    lax.fori_loop(0, n_chunks // G, group, jnp.zeros((HG_D, HG_D), F32))

    GO = min(HG_OUT_GROUP, n_chunks)
    assert n_chunks % GO == 0
    out_grp = range(GO)

    def out_group(gi, carry):
        rows = [pl.ds(pl.multiple_of((gi * GO + c) * C, C), C) for c in out_grp]
        o = [_dot_nt(qh_ref[rows[c], :], st_ref[gi * GO + c]) + oi_ref[rows[c], :] for c in out_grp]
        o = [o[c] * lax.rsqrt(jnp.mean(o[c] * o[c], axis=-1, keepdims=True) + EPS) * gout for c in out_grp]
        for c in out_grp:
            o_ref[0, rows[c], :] = (o[c] * gate_ref[0, rows[c], :].astype(F32)).astype(BF16)
        return carry

    lax.fori_loop(0, n_chunks // GO, out_group, 0)


def _hg_scan(q, k, v, lf, gate, gout):
    B, S, _ = q.shape
    a3, signs, masks = _hg_constants()
    blk = pl.BlockSpec((1, S, HG_D), lambda b, h: (b, 0, h))
    return pl.pallas_call(
        _hg_scan_kernel,
        grid=(B, HG_HEADS),
        in_specs=[blk, blk, blk, blk, blk, _const_spec(a3.shape), _const_spec(signs.shape),
                  _const_spec(masks.shape), _const_spec((1, HG_D))],
        out_specs=blk,
        out_shape=jax.ShapeDtypeStruct((B, S, HG_HEADS * HG_D), BF16),
        scratch_shapes=[pltpu.VMEM((S, HG_D), BF16),
                        pltpu.VMEM((S, HG_D), F32),
                        pltpu.VMEM((S // HG_CHUNK, HG_D, HG_D), BF16)],
        compiler_params=_cparams("parallel", "parallel"),
        name="hgrn_scan",
    )(q, k, v, lf, gate, a3, signs, masks, gout)


def _cv_glu_kernel(x_ref, g_ref, w_ref, b_ref, u_ref):
    h = _rms(x_ref[...], g_ref[...]).astype(BF16)
    a = _dot(h, w_ref[:, :D_MODEL]) + b_ref[:, :D_MODEL]
    gate = _dot(h, w_ref[:, D_MODEL:]) + b_ref[:, D_MODEL:]
    u_ref[...] = a * jax.nn.sigmoid(gate)


def _cv_glu(x2, g, w, b):
    T = x2.shape[0]
    tm = TM_PROJ
    return pl.pallas_call(
        _cv_glu_kernel,
        grid=(T // tm,),
        in_specs=[_row_spec(tm, D_MODEL), _const_spec((1, D_MODEL)), _resident_spec(w.shape),
                  _const_spec((1, 2 * D_MODEL))],
        out_specs=_row_spec(tm, D_MODEL),
        out_shape=jax.ShapeDtypeStruct((T, D_MODEL), F32),
        compiler_params=_cparams("parallel"),
        name="conv_glu",
    )(x2, g, w, b)


def _cv_dw_kernel(u_ref, prev_ref, w_ref, b_ref, lng_ref, lnb_ref, o_ref, buf_ref, sh_ref, acc_ref, *, tm):
    i = pl.program_id(1)
    halo = prev_ref[0]
    buf_ref[0:CONV_HALO, :] = jnp.where(i == 0, 0.0, halo)
    buf_ref[CONV_HALO:, :] = u_ref[0]
    n_sh = sh_ref.shape[1]
    for r in range(1, SUBLANES):
        sh_ref[r - 1] = buf_ref[r:r + n_sh, :]
    off = CONV_HALO - (CONV_WIDTH - 1)
    rb = CONV_ROWS
    for ct in range(D_MODEL // LANES):
        cols = slice(ct * LANES, (ct + 1) * LANES)
        for j0 in range(0, CONV_WIDTH, CONV_TAP_GROUP):
            group = range(j0, min(j0 + CONV_TAP_GROUP, CONV_WIDTH))
            taps = {j: w_ref[j:j + 1, cols] for j in group}
            for base in range(0, tm, rb):
                acc = jnp.zeros((rb, LANES), F32) + b_ref[:, cols] if j0 == 0 else acc_ref[base:base + rb, cols]
                for j in group:
                    a, r = divmod(off + j, SUBLANES)
                    lo = base + a * SUBLANES
                    src = buf_ref[lo:lo + rb, cols] if r == 0 else sh_ref[r - 1, lo:lo + rb, cols]
                    acc = acc + src * taps[j]
                acc_ref[base:base + rb, cols] = acc
    acc = acc_ref[...]
    mu = jnp.mean(acc, axis=-1, keepdims=True)
    xc = acc - mu
    y = xc * lax.rsqrt(jnp.mean(xc * xc, axis=-1, keepdims=True) + EPS) * lng_ref[...] + lnb_ref[...]
    o_ref[0] = (y * jax.nn.sigmoid(y)).astype(BF16)


def _cv_dw(u, w, b, lng, lnb):
    B, S, _ = u.shape
    tm = TM_CONV
    ratio = tm // CONV_HALO
    return pl.pallas_call(
        functools.partial(_cv_dw_kernel, tm=tm),
        grid=(B, S // tm),
        in_specs=[pl.BlockSpec((1, tm, D_MODEL), lambda b, i: (b, i, 0)),
                  pl.BlockSpec((1, CONV_HALO, D_MODEL), lambda b, i: (b, jnp.maximum(i * ratio - 1, 0), 0)),
                  _const_spec((CONV_WIDTH, D_MODEL)), _const_spec((1, D_MODEL)),
                  _const_spec((1, D_MODEL)), _const_spec((1, D_MODEL))],
        out_specs=pl.BlockSpec((1, tm, D_MODEL), lambda b, i: (b, i, 0)),
        out_shape=jax.ShapeDtypeStruct((B, S, D_MODEL), BF16),
        scratch_shapes=[pltpu.VMEM((tm + CONV_HALO, D_MODEL), F32),
                        pltpu.VMEM((SUBLANES - 1, tm + CONV_HALO - SUBLANES, D_MODEL), F32),
                        pltpu.VMEM((tm, D_MODEL), F32)],
        compiler_params=_cparams("parallel", "parallel"),
        name="conv_depthwise",
    )(u, u, w, b, lng, lnb)


def _dup_rope(t):
    return jnp.concatenate([t, t], axis=-1)


def _mla_weights(w_down, w_uq, q_head_norm, k_head_norm):
    wd = jnp.concatenate([w_down, w_down[:, -MLA_ROPE:]], axis=1).astype(BF16)
    wq = w_uq.reshape(MLA_Q_LORA, MLA_HEADS, MLA_QK)
    wq = jnp.concatenate([wq, wq[:, :, MLA_NOPE:]], axis=-1)
    wq = wq.reshape(MLA_Q_LORA, MLA_HEADS * HEAD_PAD).astype(BF16)
    qng = q_head_norm[None, :MLA_NOPE]
    qrg = _dup_rope(q_head_norm[None, MLA_NOPE:])
    kng = k_head_norm[None, :MLA_NOPE]
    krg = _dup_rope(k_head_norm[None, MLA_NOPE:])
    return wd, wq, qng, qrg, kng, krg


def kernel(x, positions, norm_mix, norm_mlp, mlp_w_in, mlp_w_out, mla_w_down, mla_q_lat_norm, mla_kv_lat_norm, mla_w_uq, mla_w_ukv, mla_q_head_norm, mla_k_head_norm, mla_w_o, hg_w_in, hg_lb_logits, hg_out_norm, hg_w_o, cv_w_pw1, cv_b_pw1, cv_w_dw, cv_b_dw, cv_ln_g, cv_ln_b, cv_w_pw2, cv_b_pw2):
    B, S, D = x.shape
    T = B * S
    depth = norm_mix.shape[0]
    x2 = x.reshape(T, D)

    inv_freq = jnp.power(ROPE_THETA, -jnp.arange(0, MLA_ROPE, 2, dtype=F32) / MLA_ROPE)
    invf4 = jnp.tile(inv_freq, 4)[None, :]
    cos, sin = _rope_tables(positions.astype(F32).reshape(T, 1), invf4)
    lb_table = _hg_lb_table(hg_lb_logits.astype(F32))
    zero_bias = jnp.zeros((1, D), F32)

    for layer in range(depth):
        kind, j = layer % N_MIXERS, layer // N_MIXERS
        g_mix = norm_mix[layer][None, :]
        if kind == 0:
            wd, wq, qng, qrg, kng, krg = _mla_weights(mla_w_down[j], mla_w_uq[j], mla_q_head_norm[j],
                                                      mla_k_head_norm[j])
            q, kn, kr, v = _mla_proj(x2, S, g_mix, wd, mla_q_lat_norm[j][None, :], mla_kv_lat_norm[j][None, :],
                                     wq, mla_w_ukv[j].astype(BF16), qng, qrg, kng, krg, cos, sin)
            m = _attention(q, kn, kr.reshape(B, S, -1), v).reshape(T, D)
            w_o, b_o = mla_w_o[j], zero_bias
        elif kind == 1:
            q, k, v, lf, gate = _hg_proj(x2, g_mix, hg_w_in[j].astype(BF16), lb_table[layer][None, :])
            r3 = lambda t: t.reshape(B, S, -1)
            m = _hg_scan(r3(q), r3(k), r3(v), r3(lf), r3(gate), hg_out_norm[j][None, :]).reshape(T, D)
            w_o, b_o = hg_w_o[j], zero_bias
        else:
            u = _cv_glu(x2, g_mix, cv_w_pw1[j].astype(BF16), cv_b_pw1[j][None, :])
            m = _cv_dw(u.reshape(B, S, D), cv_w_dw[j], cv_b_dw[j][None, :], cv_ln_g[j][None, :],
                       cv_ln_b[j][None, :]).reshape(T, D)
            w_o, b_o = cv_w_pw2[j], cv_b_pw2[j][None, :]
        x2 = _tail(x2, m, w_o.astype(BF16), b_o, norm_mlp[layer][None, :],
                   mlp_w_in[layer].astype(BF16), mlp_w_out[layer].astype(BF16))
    return x2.reshape(B, S, D)
```

```python
import functools

import numpy as np
import jax
import jax.numpy as jnp
from jax import lax
from jax.experimental import pallas as pl
from jax.experimental.pallas import tpu as pltpu

F32 = jnp.float32
BF16 = jnp.bfloat16

D_MODEL = 1024
N_MIXERS = 3
MLA_HEADS = 8
MLA_Q_LORA = 384
MLA_KV_LORA = 256
MLA_NOPE = 128
MLA_ROPE = 64
MLA_V = 128
MLA_QK = MLA_NOPE + MLA_ROPE
ROPE_THETA = 10000.0
HG_HEADS = 8
HG_D = D_MODEL // HG_HEADS
HG_CHUNK = 64
CONV_WIDTH = 31
D_FF = 4 * D_MODEL
EPS = 1e-6
LOG2_E = 1.4426950408889634

LANES = 128
HEAD_PAD = 2 * LANES
VMEM_LIMIT = 56 * 1024 * 1024

TM_PROJ = 1024
TM_TAIL = 1024
TF_TAIL = 512
TQ_ATTN = 512
ATTN_AHEAD = 2
ATTN_ONES_ROWS = 16
TM_CONV = 512
CONV_HALO = 32
CONV_ROWS = 64
CONV_TAP_GROUP = 16
SUBLANES = 8


def _cparams(*sem):
    return pltpu.CompilerParams(dimension_semantics=sem, vmem_limit_bytes=VMEM_LIMIT)


def _rms(x, g):
    return x * lax.rsqrt(jnp.mean(x * x, axis=-1, keepdims=True) + EPS) * g


def _dot(a, b):
    return jnp.dot(a, b, preferred_element_type=F32)


def _dot_nt(a, b):
    return lax.dot_general(a, b, (((1,), (1,)), ((), ())), preferred_element_type=F32)


def _row_spec(tm, n):
    return pl.BlockSpec((tm, n), lambda i: (i, 0))


def _const_spec(shape):
    return pl.BlockSpec(shape, lambda *_: (0,) * len(shape))


def _resident_spec(shape):
    return pl.BlockSpec(shape, lambda *_: (0,) * len(shape), pipeline_mode=pl.Buffered(1))


def _rope_table_kernel(pos_ref, invf_ref, cos_ref, sin_ref):
    ang = pos_ref[...] * invf_ref[...]
    lane = lax.broadcasted_iota(jnp.int32, (1, LANES), 1)
    sign = jnp.where((lane // (MLA_ROPE // 2)) % 2 == 0, -1.0, 1.0)
    cos_ref[...] = jnp.cos(ang)
    sin_ref[...] = jnp.sin(ang) * sign


def _rope_tables(pos_col, invf4):
    T = pos_col.shape[0]
    tm = TM_PROJ
    return pl.pallas_call(
        _rope_table_kernel,
        grid=(T // tm,),
        in_specs=[_row_spec(tm, 1), _const_spec((1, LANES))],
        out_specs=[_row_spec(tm, LANES), _row_spec(tm, LANES)],
        out_shape=[jax.ShapeDtypeStruct((T, LANES), F32)] * 2,
        compiler_params=_cparams("parallel"),
        name="rope_tables",
    )(pos_col, invf4)


def _mla_proj_kernel(x_ref, g_ref, wd_ref, qlg_ref, kvlg_ref, wuq_ref, wukv_ref,
                     qng_ref, qrg_ref, kng_ref, krg_ref, cos_ref, sin_ref, ones_ref,
                     q_ref, kn_ref, kr_ref, v_ref):
    h = _rms(x_ref[...], g_ref[...]).astype(BF16)
    lat = _dot(h, wd_ref[...])
    c_q = _rms(lat[:, :MLA_Q_LORA], qlg_ref[...]).astype(BF16)
    c_kv = _rms(lat[:, MLA_Q_LORA:MLA_Q_LORA + MLA_KV_LORA], kvlg_ref[...]).astype(BF16)
    cos = cos_ref[...]
    sin = sin_ref[...]

    def rope(r):
        return r * cos + pltpu.roll(r, MLA_ROPE // 2, 1) * sin

    ones2 = ones_ref[...]
    ones1 = ones_ref[:LANES, :LANES]

    def group_rms(t, g, ones):
        return t * lax.rsqrt(_dot((t * t).astype(BF16), ones) + EPS) * g

    lane = lax.broadcasted_iota(jnp.int32, (1, LANES), 1)
    kr = rope(group_rms(lat[:, MLA_Q_LORA + MLA_KV_LORA:], krg_ref[...], ones1))
    kr_ref[...] = jnp.where(lane < MLA_ROPE, kr, 0.0).astype(BF16)

    scale = MLA_QK ** -0.5 * LOG2_E
    qg = jnp.concatenate([qng_ref[...], qrg_ref[...]], axis=1) * scale
    kng2 = jnp.concatenate([kng_ref[...], kng_ref[...]], axis=1)
    heads = range(MLA_HEADS)
    q = [_dot(c_q, wuq_ref[:, hh * HEAD_PAD:(hh + 1) * HEAD_PAD]) for hh in heads]
    kv = [_dot(c_kv, wukv_ref[:, hh * 2 * LANES:(hh + 1) * 2 * LANES]) for hh in heads]
    for hh in heads:
        v_ref[:, hh * LANES:(hh + 1) * LANES] = kv[hh][:, LANES:].astype(BF16)
    q = [group_rms(q[hh], qg, ones2) for hh in heads]
    for hh in range(0, MLA_HEADS, 2):
        kn = jnp.concatenate([kv[hh][:, :LANES], kv[hh + 1][:, :LANES]], axis=1)
        kn_ref[:, hh * LANES:(hh + 2) * LANES] = group_rms(kn, kng2, ones2).astype(BF16)
    for hh in heads:
        q_ref[:, hh * HEAD_PAD:hh * HEAD_PAD + LANES] = q[hh][:, :LANES].astype(BF16)
        q_ref[:, hh * HEAD_PAD + LANES:(hh + 1) * HEAD_PAD] = rope(q[hh][:, LANES:]).astype(BF16)


def _mla_proj(x2, g, wd, qlg, kvlg, wuq, wukv, qng, qrg, kng, krg, cos, sin):
    T = x2.shape[0]
    tm = TM_PROJ
    group = np.arange(HEAD_PAD) // LANES
    ones = jnp.asarray((group[:, None] == group[None, :]) / LANES, BF16)
    return pl.pallas_call(
        _mla_proj_kernel,
        grid=(T // tm,),
        in_specs=[_row_spec(tm, D_MODEL), _const_spec((1, D_MODEL)), _resident_spec(wd.shape),
                  _const_spec((1, MLA_Q_LORA)), _const_spec((1, MLA_KV_LORA)),
                  _resident_spec(wuq.shape), _resident_spec(wukv.shape),
                  _const_spec((1, LANES)), _const_spec((1, LANES)), _const_spec((1, LANES)),
                  _const_spec((1, LANES)), _row_spec(tm, LANES), _row_spec(tm, LANES),
                  _const_spec((HEAD_PAD, HEAD_PAD))],
        out_specs=[_row_spec(tm, MLA_HEADS * HEAD_PAD), _row_spec(tm, MLA_HEADS * LANES),
                   _row_spec(tm, LANES), _row_spec(tm, MLA_HEADS * LANES)],
        out_shape=[jax.ShapeDtypeStruct((T, MLA_HEADS * HEAD_PAD), BF16),
                   jax.ShapeDtypeStruct((T, MLA_HEADS * LANES), BF16),
                   jax.ShapeDtypeStruct((T, LANES), BF16),
                   jax.ShapeDtypeStruct((T, MLA_HEADS * LANES), BF16)],
        compiler_params=_cparams("parallel"),
        name="mla_proj",
    )(x2, g, wd, qlg, kvlg, wuq, wukv, qng, qrg, kng, krg, cos, sin, ones)


def _attn_kernel(q_ref, kn_ref, kr_ref, v_ref, o_ref, kcat_ref, vt_ref, *, tq):
    n_q = q_ref.shape[1] // tq
    kcat_ref[:, :LANES] = kn_ref[0]
    kcat_ref[:, LANES:] = kr_ref[0]
    vt_ref[:MLA_V, :] = v_ref[0].astype(F32).T.astype(BF16)
    vt_ref[MLA_V:, :] = jnp.ones((ATTN_ONES_ROWS, vt_ref.shape[1]), BF16)

    def scores(i, j):
        return _dot_nt(kcat_ref[j * tq:(j + 1) * tq, :], q_ref[0, i * tq:(i + 1) * tq, :])

    def accumulate(j, s, carry, masked):
        m, acc = carry
        if masked:
            key = lax.broadcasted_iota(jnp.int32, (tq, tq), 0)
            qry = lax.broadcasted_iota(jnp.int32, (tq, tq), 1)
            s = jnp.where(key <= qry, s, -jnp.inf)
        m_new = jnp.maximum(m, jnp.max(s, axis=0, keepdims=True))
        p = jnp.exp2(s - m_new)
        acc = jnp.exp2(m - m_new) * acc + _dot(vt_ref[:, j * tq:(j + 1) * tq], p.astype(BF16))
        return m_new, acc

    pairs = [(i, j) for i in range(n_q) for j in range(i + 1)]
    queue = [scores(*pairs[n]) for n in range(min(ATTN_AHEAD, len(pairs)))]
    carry = None
    for n, (i, j) in enumerate(pairs):
        if n + ATTN_AHEAD < len(pairs):
            queue.append(scores(*pairs[n + ATTN_AHEAD]))
        s = queue.pop(0)
        if j == 0:
            carry = (jnp.full((1, tq), -jnp.inf, F32), jnp.zeros((MLA_V + ATTN_ONES_ROWS, tq), F32))
        carry = accumulate(j, s, carry, masked=(j == i))
        if j == i:
            acc = carry[1]
            o_ref[0, i * tq:(i + 1) * tq, :] = (acc[:MLA_V] / acc[MLA_V:MLA_V + 1]).T.astype(BF16)


def _attention(q, kn, kr, v):
    B, S, _ = q.shape
    return pl.pallas_call(
        functools.partial(_attn_kernel, tq=TQ_ATTN),
        grid=(B, MLA_HEADS),
        in_specs=[pl.BlockSpec((1, S, HEAD_PAD), lambda b, h: (b, 0, h)),
                  pl.BlockSpec((1, S, LANES), lambda b, h: (b, 0, h)),
                  pl.BlockSpec((1, S, LANES), lambda b, h: (b, 0, 0)),
                  pl.BlockSpec((1, S, LANES), lambda b, h: (b, 0, h))],
        out_specs=pl.BlockSpec((1, S, MLA_V), lambda b, h: (b, 0, h)),
        out_shape=jax.ShapeDtypeStruct((B, S, MLA_HEADS * MLA_V), BF16),
        scratch_shapes=[pltpu.VMEM((S, HEAD_PAD), BF16), pltpu.VMEM((MLA_V + ATTN_ONES_ROWS, S), BF16)],
        compiler_params=_cparams("parallel", "parallel"),
        name="mla_attention",
    )(q, kn, kr, v)


def _tail_kernel(x_ref, m_ref, wo_ref, bo_ref, g_ref, win_ref, wout_ref, o_ref):
    x1 = x_ref[...] + _dot(m_ref[...], wo_ref[...]) + bo_ref[...]
    h = _rms(x1, g_ref[...]).astype(BF16)
    acc = x1
    for kk in range(D_FF // TF_TAIL):
        a = jnp.maximum(_dot(h, win_ref[:, kk * TF_TAIL:(kk + 1) * TF_TAIL]), 0.0)
        acc = acc + _dot((a * a).astype(BF16), wout_ref[kk * TF_TAIL:(kk + 1) * TF_TAIL, :])
    o_ref[...] = acc


def _tail(x2, m2, wo, bo, g, win, wout):
    T = x2.shape[0]
    tm = TM_TAIL
    return pl.pallas_call(
        _tail_kernel,
        grid=(T // tm,),
        in_specs=[_row_spec(tm, D_MODEL), _row_spec(tm, D_MODEL), _resident_spec(wo.shape),
                  _const_spec((1, D_MODEL)), _const_spec((1, D_MODEL)),
                  _resident_spec(win.shape), _resident_spec(wout.shape)],
        out_specs=_row_spec(tm, D_MODEL),
        out_shape=jax.ShapeDtypeStruct((T, D_MODEL), F32),
        compiler_params=_cparams("parallel"),
        name="tail_mlp",
    )(x2, m2, wo, bo, g, win, wout)


def _hg_lb_kernel(logits_ref, lb_ref):
    z = logits_ref[...]
    e = jnp.exp(z - jnp.max(z, axis=0, keepdims=True))
    sm = e / jnp.sum(e, axis=0, keepdims=True)
    depth = z.shape[0]
    acc = sm[0:1]
    first = acc
    lb_ref[0:1, :] = acc - first
    for i in range(1, depth):
        acc = acc + sm[i:i + 1]
        lb_ref[i:i + 1, :] = acc - first


def _hg_lb_table(logits):
    return pl.pallas_call(
        _hg_lb_kernel,
        out_shape=jax.ShapeDtypeStruct(logits.shape, F32),
        name="hgrn_lower_bounds",
    )(logits)


def _hg_proj_kernel(x_ref, g_ref, w_ref, lb_ref, q_ref, k_ref, v_ref, lf_ref, gate_ref):
    h = _rms(x_ref[...], g_ref[...]).astype(BF16)
    lb = lb_ref[...]
    n = HG_HEADS * HG_D
    fz = _dot(h, w_ref[:, n:2 * n])
    gz = _dot(h, w_ref[:, 3 * n:4 * n])
    q_ref[...] = _dot(h, w_ref[:, 0:n]).astype(BF16)
    lf_ref[...] = jnp.log2(lb + (1.0 - lb) * jax.nn.sigmoid(fz))
    k_ref[...] = ((1.0 - lb) * jax.nn.sigmoid(-fz)).astype(BF16)
    v_ref[...] = _dot(h, w_ref[:, 2 * n:3 * n]).astype(BF16)
    gate_ref[...] = (gz * jax.nn.sigmoid(gz)).astype(BF16)


def _hg_proj(x2, g, w, lb):
    T = x2.shape[0]
    tm = TM_PROJ
    n = HG_HEADS * HG_D
    bf = jax.ShapeDtypeStruct((T, n), BF16)
    return pl.pallas_call(
        _hg_proj_kernel,
        grid=(T // tm,),
        in_specs=[_row_spec(tm, D_MODEL), _const_spec((1, D_MODEL)), _resident_spec(w.shape),
                  _const_spec((1, n))],
        out_specs=[_row_spec(tm, n)] * 5,
        out_shape=[bf, bf, bf, jax.ShapeDtypeStruct((T, n), F32), bf],
        compiler_params=_cparams("parallel"),
        name="hgrn_proj",
    )(x2, g, w, lb)


HG_LEVELS = 6
HG_COARSE = 3
HG_GROUP = 32
HG_OUT_GROUP = 32


def _hg_level(lvl):
    C = HG_CHUNK
    t = np.arange(C)
    half = (C // 2) >> lvl
    blk = t // (2 * half)
    second = (t % (2 * half)) >= half
    mid = blk * 2 * half + half - 1
    return half, blk, second, mid


def _hg_constants():
    C = HG_CHUNK
    t = np.arange(C)
    j = t[None, :]
    mats = [j <= t[:, None]]
    signs, masks = [], []
    for lvl in range(HG_LEVELS):
        half, blk, second, mid = _hg_level(lvl)
        if lvl < HG_COARSE:
            signs.append(np.where(second, 1.0, -1.0)[:, None] * np.ones((1, HG_D)))
        else:
            a_q = (j > mid[:, None]) & (j <= t[:, None])
            a_k = (j > t[:, None]) & (j <= mid[:, None])
            mats.append(np.where(second[:, None], a_q, a_k))
        masks.append((blk[:, None] == blk[None, :]) & second[:, None] & (~second)[None, :])
    masks.append(np.eye(C, dtype=bool))
    a = np.concatenate(mats, axis=0).astype(np.float32)
    a3 = np.concatenate([a, a, a], axis=1)
    return (jnp.asarray(a3, BF16), jnp.asarray(np.stack(signs), F32),
            jnp.asarray(np.stack(masks).astype(np.float32)))


def _hg_scan_kernel(q_ref, k_ref, v_ref, lf_ref, gate_ref, a3_ref, sign_ref, mask_ref, gout_ref, o_ref,
                    qh_ref, oi_ref, st_ref):
    C = HG_CHUNK
    n_chunks = q_ref.shape[1] // C
    gout = gout_ref[...]

    G = min(HG_GROUP, n_chunks)
    assert n_chunks % G == 0
    grp = range(G)

    def group(gi, state):
        rows = [pl.ds(pl.multiple_of((gi * G + c) * C, C), C) for c in grp]
        lf3 = []
        for c in grp:
            lf = lf_ref[0, rows[c], :]
            hi = lf.astype(BF16)
            r1 = lf - hi.astype(F32)
            mid = r1.astype(BF16)
            lo = (r1 - mid.astype(F32)).astype(BF16)
            lf3.append(jnp.concatenate([hi, mid, lo], axis=0))
        seg = _dot(a3_ref[...], jnp.concatenate(lf3, axis=1))
        seg = [seg[:, c * HG_D:(c + 1) * HG_D] for c in grp]
        b = [seg[c][0:C] for c in grp]
        q = [q_ref[0, rows[c], :].astype(F32) for c in grp]
        k = [k_ref[0, rows[c], :].astype(F32) for c in grp]
        v = [v_ref[0, rows[c], :] for c in grp]
        attn = [_dot_nt(q[c].astype(BF16), k[c].astype(BF16)) * mask_ref[HG_LEVELS] for c in grp]
        upd = [lax.dot_general(v[c], (k[c] * jnp.exp2(b[c][C - 1:C] - b[c])).astype(BF16),
                               (((0,), (0,)), ((), ())), preferred_element_type=F32) for c in grp]
        for c in grp:
            qh_ref[rows[c], :] = (q[c] * jnp.exp2(b[c])).astype(BF16)
        for lvl in range(HG_LEVELS):
            for c in grp:
                if lvl < HG_COARSE:
                    half, _, _, mids = _hg_level(lvl)
                    b_mid = jnp.concatenate([jnp.broadcast_to(b[c][m:m + 1], (2 * half, HG_D))
                                             for m in mids[::2 * half]], axis=0)
                    e = (b[c] - b_mid) * sign_ref[lvl]
                else:
                    e = seg[c][(1 + lvl - HG_COARSE) * C:(2 + lvl - HG_COARSE) * C]
                xl = jnp.exp2(e)
                attn[c] = attn[c] + _dot_nt((q[c] * xl).astype(BF16), (k[c] * xl).astype(BF16)) * mask_ref[lvl]
        for c in grp:
            oi_ref[rows[c], :] = _dot(attn[c].astype(BF16), v[c])
            st_ref[gi * G + c] = state.astype(BF16)
            state = state * jnp.exp2(b[c][C - 1:C]) + upd[c]
        return state

    lax.fori_loop(0, n_chunks // G, group, jnp.zeros((HG_D, HG_D), F32))

    GO = min(HG_OUT_GROUP, n_chunks)
    assert n_chunks % GO == 0
    out_grp = range(GO)

    def out_group(gi, carry):
        rows = [pl.ds(pl.multiple_of((gi * GO + c) * C, C), C) for c in out_grp]
        o = [_dot_nt(qh_ref[rows[c], :], st_ref[gi * GO + c]) + oi_ref[rows[c], :] for c in out_grp]
        o = [o[c] * lax.rsqrt(jnp.mean(o[c] * o[c], axis=-1, keepdims=True) + EPS) * gout for c in out_grp]
        for c in out_grp:
            o_ref[0, rows[c], :] = (o[c] * gate_ref[0, rows[c], :].astype(F32)).astype(BF16)
        return carry

    lax.fori_loop(0, n_chunks // GO, out_group, 0)


def _hg_scan(q, k, v, lf, gate, gout):
    B, S, _ = q.shape
    a3, signs, masks = _hg_constants()
    blk = pl.BlockSpec((1, S, HG_D), lambda b, h: (b, 0, h))
    return pl.pallas_call(
        _hg_scan_kernel,
        grid=(B, HG_HEADS),
        in_specs=[blk, blk, blk, blk, blk, _const_spec(a3.shape), _const_spec(signs.shape),
                  _const_spec(masks.shape), _const_spec((1, HG_D))],
        out_specs=blk,
        out_shape=jax.ShapeDtypeStruct((B, S, HG_HEADS * HG_D), BF16),
        scratch_shapes=[pltpu.VMEM((S, HG_D), BF16),
                        pltpu.VMEM((S, HG_D), F32),
                        pltpu.VMEM((S // HG_CHUNK, HG_D, HG_D), BF16)],
        compiler_params=_cparams("parallel", "parallel"),
        name="hgrn_scan",
    )(q, k, v, lf, gate, a3, signs, masks, gout)


def _cv_glu_kernel(x_ref, g_ref, w_ref, b_ref, u_ref):
    h = _rms(x_ref[...], g_ref[...]).astype(BF16)
    a = _dot(h, w_ref[:, :D_MODEL]) + b_ref[:, :D_MODEL]
    gate = _dot(h, w_ref[:, D_MODEL:]) + b_ref[:, D_MODEL:]
    u_ref[...] = a * jax.nn.sigmoid(gate)


def _cv_glu(x2, g, w, b):
    T = x2.shape[0]
    tm = TM_PROJ
    return pl.pallas_call(
        _cv_glu_kernel,
        grid=(T // tm,),
        in_specs=[_row_spec(tm, D_MODEL), _const_spec((1, D_MODEL)), _resident_spec(w.shape),
                  _const_spec((1, 2 * D_MODEL))],
        out_specs=_row_spec(tm, D_MODEL),
        out_shape=jax.ShapeDtypeStruct((T, D_MODEL), F32),
        compiler_params=_cparams("parallel"),
        name="conv_glu",
    )(x2, g, w, b)


def _conv_lane_tile(ct, tm, buf_ref, sh_ref, acc_ref, w_ref, b_ref):
    off = CONV_HALO - (CONV_WIDTH - 1)
    rb = CONV_ROWS
    cols = slice(ct * LANES, (ct + 1) * LANES)
    for j0 in range(0, CONV_WIDTH, CONV_TAP_GROUP):
        group = range(j0, min(j0 + CONV_TAP_GROUP, CONV_WIDTH))
        taps = {j: w_ref[j:j + 1, cols] for j in group}
        for base in range(0, tm, rb):
            acc = jnp.zeros((rb, LANES), F32) + b_ref[:, cols] if j0 == 0 else acc_ref[base:base + rb, cols]
            for j in group:
                a, r = divmod(off + j, SUBLANES)
                lo = base + a * SUBLANES
                src = buf_ref[lo:lo + rb, cols] if r == 0 else sh_ref[r - 1, lo:lo + rb, cols]
                acc = acc + src * taps[j]
            acc_ref[base:base + rb, cols] = acc


def _ln_swish(acc, g, b):
    mu = jnp.mean(acc, axis=-1, keepdims=True)
    xc = acc - mu
    y = xc * lax.rsqrt(jnp.mean(xc * xc, axis=-1, keepdims=True) + EPS) * g + b
    return (y * jax.nn.sigmoid(y)).astype(BF16)


def _conv_tail_kernel(u_ref, prev_ref, x_ref, w_ref, b_ref, lng_ref, lnb_ref, wo_ref, bo_ref, g_ref, win_ref,
                      wout_ref, o_ref, buf_ref, sh_ref, acc_ref, m_ref, *, tm, n_tiles, tiles_per_seq):
    i = pl.program_id(0)

    @pl.when(i == 0)
    def _():
        m_ref[...] = jnp.zeros_like(m_ref)

    tile = jnp.minimum(i, n_tiles - 1)
    buf_ref[0:CONV_HALO, :] = jnp.where(tile % tiles_per_seq == 0, 0.0, prev_ref[...])
    buf_ref[CONV_HALO:, :] = u_ref[...]
    n_sh = sh_ref.shape[1]
    for r in range(1, SUBLANES):
        sh_ref[r - 1] = buf_ref[r:r + n_sh, :]

    x1 = x_ref[...] + _dot(m_ref[(i + 1) % 2], wo_ref[...]) + bo_ref[...]
    h = _rms(x1, g_ref[...]).astype(BF16)
    out = x1
    n_ff = D_FF // TF_TAIL
    n_ct = D_MODEL // LANES
    for kk in range(n_ff):
        a = jnp.maximum(_dot(h, win_ref[:, kk * TF_TAIL:(kk + 1) * TF_TAIL]), 0.0)
        out = out + _dot((a * a).astype(BF16), wout_ref[kk * TF_TAIL:(kk + 1) * TF_TAIL, :])
        for ct in range(kk * n_ct // n_ff, (kk + 1) * n_ct // n_ff):
            _conv_lane_tile(ct, tm, buf_ref, sh_ref, acc_ref, w_ref, b_ref)
    o_ref[...] = out
    m_ref[i % 2] = _ln_swish(acc_ref[...], lng_ref[...], lnb_ref[...])


def _conv_tail(u, x2, seq_len, w, b, lng, lnb, wo, bo, g, win, wout):
    T = x2.shape[0]
    tm = TM_CONV
    n_tiles = T // tm
    ratio = tm // CONV_HALO
    last = n_tiles - 1
    return pl.pallas_call(
        functools.partial(_conv_tail_kernel, tm=tm, n_tiles=n_tiles, tiles_per_seq=seq_len // tm),
        grid=(n_tiles + 1,),
        in_specs=[pl.BlockSpec((tm, D_MODEL), lambda i: (jnp.minimum(i, last), 0)),
                  pl.BlockSpec((CONV_HALO, D_MODEL), lambda i: (jnp.maximum(jnp.minimum(i, last) * ratio - 1, 0), 0)),
                  pl.BlockSpec((tm, D_MODEL), lambda i: (jnp.maximum(i - 1, 0), 0)),
                  _const_spec((CONV_WIDTH, D_MODEL)), _const_spec((1, D_MODEL)),
                  _const_spec((1, D_MODEL)), _const_spec((1, D_MODEL)),
                  _resident_spec(wo.shape), _const_spec((1, D_MODEL)), _const_spec((1, D_MODEL)),
                  _resident_spec(win.shape), _resident_spec(wout.shape)],
        out_specs=pl.BlockSpec((tm, D_MODEL), lambda i: (jnp.maximum(i - 1, 0), 0)),
        out_shape=jax.ShapeDtypeStruct((T, D_MODEL), F32),
        scratch_shapes=[pltpu.VMEM((tm + CONV_HALO, D_MODEL), F32),
                        pltpu.VMEM((SUBLANES - 1, tm + CONV_HALO - SUBLANES, D_MODEL), F32),
                        pltpu.VMEM((tm, D_MODEL), F32),
                        pltpu.VMEM((2, tm, D_MODEL), BF16)],
        compiler_params=_cparams("arbitrary"),
        name="conv_tail",
    )(u, u, x2, w, b, lng, lnb, wo, bo, g, win, wout)


def _dup_rope(t):
    return jnp.concatenate([t, t], axis=-1)


def _mla_weights(w_down, w_uq, q_head_norm, k_head_norm):
    wd = jnp.concatenate([w_down, w_down[:, -MLA_ROPE:]], axis=1).astype(BF16)
    wq = w_uq.reshape(MLA_Q_LORA, MLA_HEADS, MLA_QK)
    wq = jnp.concatenate([wq, wq[:, :, MLA_NOPE:]], axis=-1)
    wq = wq.reshape(MLA_Q_LORA, MLA_HEADS * HEAD_PAD).astype(BF16)
    qng = q_head_norm[None, :MLA_NOPE]
    qrg = _dup_rope(q_head_norm[None, MLA_NOPE:])
    kng = k_head_norm[None, :MLA_NOPE]
    krg = _dup_rope(k_head_norm[None, MLA_NOPE:])
    return wd, wq, qng, qrg, kng, krg


def kernel(x, positions, norm_mix, norm_mlp, mlp_w_in, mlp_w_out, mla_w_down, mla_q_lat_norm, mla_kv_lat_norm, mla_w_uq, mla_w_ukv, mla_q_head_norm, mla_k_head_norm, mla_w_o, hg_w_in, hg_lb_logits, hg_out_norm, hg_w_o, cv_w_pw1, cv_b_pw1, cv_w_dw, cv_b_dw, cv_ln_g, cv_ln_b, cv_w_pw2, cv_b_pw2):
    B, S, D = x.shape
    T = B * S
    depth = norm_mix.shape[0]
    x2 = x.reshape(T, D)

    inv_freq = jnp.power(ROPE_THETA, -jnp.arange(0, MLA_ROPE, 2, dtype=F32) / MLA_ROPE)
    invf4 = jnp.tile(inv_freq, 4)[None, :]
    cos, sin = _rope_tables(positions.astype(F32).reshape(T, 1), invf4)
    lb_table = _hg_lb_table(hg_lb_logits.astype(F32))
    zero_bias = jnp.zeros((1, D), F32)

    for layer in range(depth):
        kind, j = layer % N_MIXERS, layer // N_MIXERS
        g_mix = norm_mix[layer][None, :]
        if kind == 0:
            wd, wq, qng, qrg, kng, krg = _mla_weights(mla_w_down[j], mla_w_uq[j], mla_q_head_norm[j],
                                                      mla_k_head_norm[j])
            q, kn, kr, v = _mla_proj(x2, g_mix, wd, mla_q_lat_norm[j][None, :], mla_kv_lat_norm[j][None, :],
                                     wq, mla_w_ukv[j].astype(BF16), qng, qrg, kng, krg, cos, sin)
            m = _attention(q.reshape(B, S, -1), kn.reshape(B, S, -1), kr.reshape(B, S, -1),
                           v.reshape(B, S, -1)).reshape(T, D)
            w_o, b_o = mla_w_o[j], zero_bias
        elif kind == 1:
            q, k, v, lf, gate = _hg_proj(x2, g_mix, hg_w_in[j].astype(BF16), lb_table[layer][None, :])
            r3 = lambda t: t.reshape(B, S, -1)
            m = _hg_scan(r3(q), r3(k), r3(v), r3(lf), r3(gate), hg_out_norm[j][None, :]).reshape(T, D)
            w_o, b_o = hg_w_o[j], zero_bias
        else:
            u = _cv_glu(x2, g_mix, cv_w_pw1[j].astype(BF16), cv_b_pw1[j][None, :])
            x2 = _conv_tail(u, x2, S, cv_w_dw[j], cv_b_dw[j][None, :], cv_ln_g[j][None, :], cv_ln_b[j][None, :],
                            cv_w_pw2[j].astype(BF16), cv_b_pw2[j][None, :], norm_mlp[layer][None, :],
                            mlp_w_in[layer].astype(BF16), mlp_w_out[layer].astype(BF16))
            continue
        x2 = _tail(x2, m, w_o.astype(BF16), b_o, norm_mlp[layer][None, :],
                   mlp_w_in[layer].astype(BF16), mlp_w_out[layer].astype(BF16))
    return x2.reshape(B, S, D)
```

```python
import functools

import numpy as np
import jax
import jax.numpy as jnp
from jax import lax
from jax.experimental import pallas as pl
from jax.experimental.pallas import tpu as pltpu

F32 = jnp.float32
BF16 = jnp.bfloat16

D_MODEL = 1024
N_MIXERS = 3
MLA_HEADS = 8
MLA_Q_LORA = 384
MLA_KV_LORA = 256
MLA_NOPE = 128
MLA_ROPE = 64
MLA_V = 128
MLA_QK = MLA_NOPE + MLA_ROPE
ROPE_THETA = 10000.0
HG_HEADS = 8
HG_D = D_MODEL // HG_HEADS
HG_CHUNK = 64
CONV_WIDTH = 31
D_FF = 4 * D_MODEL
EPS = 1e-6
LOG2_E = 1.4426950408889634

LANES = 128
HEAD_PAD = 2 * LANES
VMEM_LIMIT = 56 * 1024 * 1024

TM_PROJ = 1024
TM_TAIL = 1024
TF_TAIL = 512
TQ_ATTN = 512
ATTN_ONES_ROWS = 16
TM_CONV = 512
CONV_HALO = 32
CONV_ROWS = 64
CONV_TAP_GROUP = 16
SUBLANES = 8


def _cparams(*sem):
    return pltpu.CompilerParams(dimension_semantics=sem, vmem_limit_bytes=VMEM_LIMIT)


def _rms(x, g):
    return x * lax.rsqrt(jnp.mean(x * x, axis=-1, keepdims=True) + EPS) * g


def _dot(a, b):
    return jnp.dot(a, b, preferred_element_type=F32)


def _dot_nt(a, b):
    return lax.dot_general(a, b, (((1,), (1,)), ((), ())), preferred_element_type=F32)


def _row_spec(tm, n):
    return pl.BlockSpec((tm, n), lambda i: (i, 0))


def _const_spec(shape):
    return pl.BlockSpec(shape, lambda *_: (0,) * len(shape))


def _resident_spec(shape):
    return pl.BlockSpec(shape, lambda *_: (0,) * len(shape), pipeline_mode=pl.Buffered(1))


def _rope_table_kernel(pos_ref, invf_ref, cos_ref, sin_ref):
    ang = pos_ref[...] * invf_ref[...]
    lane = lax.broadcasted_iota(jnp.int32, (1, LANES), 1)
    sign = jnp.where((lane // (MLA_ROPE // 2)) % 2 == 0, -1.0, 1.0)
    cos_ref[...] = jnp.cos(ang)
    sin_ref[...] = jnp.sin(ang) * sign


def _rope_tables(pos_col, invf4):
    T = pos_col.shape[0]
    tm = TM_PROJ
    return pl.pallas_call(
        _rope_table_kernel,
        grid=(T // tm,),
        in_specs=[_row_spec(tm, 1), _const_spec((1, LANES))],
        out_specs=[_row_spec(tm, LANES), _row_spec(tm, LANES)],
        out_shape=[jax.ShapeDtypeStruct((T, LANES), F32)] * 2,
        compiler_params=_cparams("parallel"),
        name="rope_tables",
    )(pos_col, invf4)


def _mla_proj_kernel(x_ref, g_ref, wd_ref, qlg_ref, kvlg_ref, wuq_ref, wukv_ref,
                     qng_ref, qrg_ref, kng_ref, krg_ref, cos_ref, sin_ref, ones_ref,
                     q_ref, kn_ref, kr_ref, v_ref):
    h = _rms(x_ref[...], g_ref[...]).astype(BF16)
    lat = _dot(h, wd_ref[...])
    c_q = _rms(lat[:, :MLA_Q_LORA], qlg_ref[...]).astype(BF16)
    c_kv = _rms(lat[:, MLA_Q_LORA:MLA_Q_LORA + MLA_KV_LORA], kvlg_ref[...]).astype(BF16)
    cos = cos_ref[...]
    sin = sin_ref[...]

    def rope(r):
        return r * cos + pltpu.roll(r, MLA_ROPE // 2, 1) * sin

    ones2 = ones_ref[...]
    ones1 = ones_ref[:LANES, :LANES]

    def group_rms(t, g, ones):
        return t * lax.rsqrt(_dot((t * t).astype(BF16), ones) + EPS) * g

    lane = lax.broadcasted_iota(jnp.int32, (1, LANES), 1)
    kr = rope(group_rms(lat[:, MLA_Q_LORA + MLA_KV_LORA:], krg_ref[...], ones1))
    kr_ref[...] = jnp.where(lane < MLA_ROPE, kr, 0.0).astype(BF16)

    scale = MLA_QK ** -0.5 * LOG2_E
    qg = jnp.concatenate([qng_ref[...], qrg_ref[...]], axis=1) * scale
    kng2 = jnp.concatenate([kng_ref[...], kng_ref[...]], axis=1)
    heads = range(MLA_HEADS)
    q = [_dot(c_q, wuq_ref[:, hh * HEAD_PAD:(hh + 1) * HEAD_PAD]) for hh in heads]
    kv = [_dot(c_kv, wukv_ref[:, hh * 2 * LANES:(hh + 1) * 2 * LANES]) for hh in heads]
    for hh in heads:
        v_ref[:, hh * LANES:(hh + 1) * LANES] = kv[hh][:, LANES:].astype(BF16)
    q = [group_rms(q[hh], qg, ones2) for hh in heads]
    for hh in range(0, MLA_HEADS, 2):
        kn = jnp.concatenate([kv[hh][:, :LANES], kv[hh + 1][:, :LANES]], axis=1)
        kn_ref[:, hh * LANES:(hh + 2) * LANES] = group_rms(kn, kng2, ones2).astype(BF16)
    for hh in heads:
        q_ref[:, hh * HEAD_PAD:hh * HEAD_PAD + LANES] = q[hh][:, :LANES].astype(BF16)
        q_ref[:, hh * HEAD_PAD + LANES:(hh + 1) * HEAD_PAD] = rope(q[hh][:, LANES:]).astype(BF16)


def _mla_proj(x2, g, wd, qlg, kvlg, wuq, wukv, qng, qrg, kng, krg, cos, sin):
    T = x2.shape[0]
    tm = TM_PROJ
    group = np.arange(HEAD_PAD) // LANES
    ones = jnp.asarray((group[:, None] == group[None, :]) / LANES, BF16)
    return pl.pallas_call(
        _mla_proj_kernel,
        grid=(T // tm,),
        in_specs=[_row_spec(tm, D_MODEL), _const_spec((1, D_MODEL)), _resident_spec(wd.shape),
                  _const_spec((1, MLA_Q_LORA)), _const_spec((1, MLA_KV_LORA)),
                  _resident_spec(wuq.shape), _resident_spec(wukv.shape),
                  _const_spec((1, LANES)), _const_spec((1, LANES)), _const_spec((1, LANES)),
                  _const_spec((1, LANES)), _row_spec(tm, LANES), _row_spec(tm, LANES),
                  _const_spec((HEAD_PAD, HEAD_PAD))],
        out_specs=[_row_spec(tm, MLA_HEADS * HEAD_PAD), _row_spec(tm, MLA_HEADS * LANES),
                   _row_spec(tm, LANES), _row_spec(tm, MLA_HEADS * LANES)],
        out_shape=[jax.ShapeDtypeStruct((T, MLA_HEADS * HEAD_PAD), BF16),
                   jax.ShapeDtypeStruct((T, MLA_HEADS * LANES), BF16),
                   jax.ShapeDtypeStruct((T, LANES), BF16),
                   jax.ShapeDtypeStruct((T, MLA_HEADS * LANES), BF16)],
        compiler_params=_cparams("parallel"),
        name="mla_proj",
    )(x2, g, wd, qlg, kvlg, wuq, wukv, qng, qrg, kng, krg, cos, sin, ones)


def _attn_kernel(q_ref, kn_ref, kr_ref, v_ref, o_ref, kcat_ref, vt_ref, *, tq):
    n_q = q_ref.shape[1] // tq
    kcat_ref[:, :LANES] = kn_ref[0]
    kcat_ref[:, LANES:] = kr_ref[0]
    vt_ref[:MLA_V, :] = v_ref[0].astype(F32).T.astype(BF16)
    vt_ref[MLA_V:, :] = jnp.ones((ATTN_ONES_ROWS, vt_ref.shape[1]), BF16)

    def scores(i, j):
        return _dot_nt(kcat_ref[j * tq:(j + 1) * tq, :], q_ref[0, i * tq:(i + 1) * tq, :])

    def accumulate(j, s, carry, masked):
        m, acc = carry
        if masked:
            key = lax.broadcasted_iota(jnp.int32, (tq, tq), 0)
            qry = lax.broadcasted_iota(jnp.int32, (tq, tq), 1)
            s = jnp.where(key <= qry, s, -jnp.inf)
        m_new = jnp.maximum(m, jnp.max(s, axis=0, keepdims=True))
        p = jnp.exp2(s - m_new)
        acc = jnp.exp2(m - m_new) * acc + _dot(vt_ref[:, j * tq:(j + 1) * tq], p.astype(BF16))
        return m_new, acc

    pairs = [(i, j) for i in range(n_q) for j in range(i + 1)]
    s_next = scores(*pairs[0])
    carry = None
    for n, (i, j) in enumerate(pairs):
        s, s_next = s_next, (scores(*pairs[n + 1]) if n + 1 < len(pairs) else None)
        if j == 0:
            carry = (jnp.full((1, tq), -jnp.inf, F32), jnp.zeros((MLA_V + ATTN_ONES_ROWS, tq), F32))
        carry = accumulate(j, s, carry, masked=(j == i))
        if j == i:
            acc = carry[1]
            o_ref[0, i * tq:(i + 1) * tq, :] = (acc[:MLA_V] / acc[MLA_V:MLA_V + 1]).T.astype(BF16)


def _attention(q, kn, kr, v):
    B, S, _ = q.shape
    return pl.pallas_call(
        functools.partial(_attn_kernel, tq=TQ_ATTN),
        grid=(B, MLA_HEADS),
        in_specs=[pl.BlockSpec((1, S, HEAD_PAD), lambda b, h: (b, 0, h)),
                  pl.BlockSpec((1, S, LANES), lambda b, h: (b, 0, h)),
                  pl.BlockSpec((1, S, LANES), lambda b, h: (b, 0, 0)),
                  pl.BlockSpec((1, S, LANES), lambda b, h: (b, 0, h))],
        out_specs=pl.BlockSpec((1, S, MLA_V), lambda b, h: (b, 0, h)),
        out_shape=jax.ShapeDtypeStruct((B, S, MLA_HEADS * MLA_V), BF16),
        scratch_shapes=[pltpu.VMEM((S, HEAD_PAD), BF16), pltpu.VMEM((MLA_V + ATTN_ONES_ROWS, S), BF16)],
        compiler_params=_cparams("parallel", "parallel"),
        name="mla_attention",
    )(q, kn, kr, v)


def _tail_kernel(x_ref, m_ref, wo_ref, bo_ref, g_ref, win_ref, wout_ref, o_ref):
    x1 = x_ref[...] + _dot(m_ref[...], wo_ref[...]) + bo_ref[...]
    h = _rms(x1, g_ref[...]).astype(BF16)
    acc = x1
    for kk in range(D_FF // TF_TAIL):
        a = jnp.maximum(_dot(h, win_ref[:, kk * TF_TAIL:(kk + 1) * TF_TAIL]), 0.0)
        acc = acc + _dot((a * a).astype(BF16), wout_ref[kk * TF_TAIL:(kk + 1) * TF_TAIL, :])
    o_ref[...] = acc


def _tail(x2, m2, wo, bo, g, win, wout):
    T = x2.shape[0]
    tm = TM_TAIL
    return pl.pallas_call(
        _tail_kernel,
        grid=(T // tm,),
        in_specs=[_row_spec(tm, D_MODEL), _row_spec(tm, D_MODEL), _resident_spec(wo.shape),
                  _const_spec((1, D_MODEL)), _const_spec((1, D_MODEL)),
                  _resident_spec(win.shape), _resident_spec(wout.shape)],
        out_specs=_row_spec(tm, D_MODEL),
        out_shape=jax.ShapeDtypeStruct((T, D_MODEL), F32),
        compiler_params=_cparams("parallel"),
        name="tail_mlp",
    )(x2, m2, wo, bo, g, win, wout)


def _hg_lb_kernel(logits_ref, lb_ref):
    z = logits_ref[...]
    e = jnp.exp(z - jnp.max(z, axis=0, keepdims=True))
    sm = e / jnp.sum(e, axis=0, keepdims=True)
    depth = z.shape[0]
    acc = sm[0:1]
    first = acc
    lb_ref[0:1, :] = acc - first
    for i in range(1, depth):
        acc = acc + sm[i:i + 1]
        lb_ref[i:i + 1, :] = acc - first


def _hg_lb_table(logits):
    return pl.pallas_call(
        _hg_lb_kernel,
        out_shape=jax.ShapeDtypeStruct(logits.shape, F32),
        name="hgrn_lower_bounds",
    )(logits)


def _hg_proj_kernel(x_ref, g_ref, w_ref, lb_ref, q_ref, k_ref, v_ref, lf_ref, gate_ref):
    h = _rms(x_ref[...], g_ref[...]).astype(BF16)
    lb = lb_ref[...]
    n = HG_HEADS * HG_D
    fz = _dot(h, w_ref[:, n:2 * n])
    gz = _dot(h, w_ref[:, 3 * n:4 * n])
    q_ref[...] = _dot(h, w_ref[:, 0:n]).astype(BF16)
    lf_ref[...] = jnp.log2(lb + (1.0 - lb) * jax.nn.sigmoid(fz))
    k_ref[...] = ((1.0 - lb) * jax.nn.sigmoid(-fz)).astype(BF16)
    v_ref[...] = _dot(h, w_ref[:, 2 * n:3 * n]).astype(BF16)
    gate_ref[...] = (gz * jax.nn.sigmoid(gz)).astype(BF16)


def _hg_proj(x2, g, w, lb):
    T = x2.shape[0]
    tm = TM_PROJ
    n = HG_HEADS * HG_D
    bf = jax.ShapeDtypeStruct((T, n), BF16)
    return pl.pallas_call(
        _hg_proj_kernel,
        grid=(T // tm,),
        in_specs=[_row_spec(tm, D_MODEL), _const_spec((1, D_MODEL)), _resident_spec(w.shape),
                  _const_spec((1, n))],
        out_specs=[_row_spec(tm, n)] * 5,
        out_shape=[bf, bf, bf, jax.ShapeDtypeStruct((T, n), F32), bf],
        compiler_params=_cparams("parallel"),
        name="hgrn_proj",
    )(x2, g, w, lb)


HG_LEVELS = 6
HG_COARSE = 3
HG_GROUP = 32
HG_OUT_GROUP = 32


def _hg_level(lvl):
    C = HG_CHUNK
    t = np.arange(C)
    half = (C // 2) >> lvl
    blk = t // (2 * half)
    second = (t % (2 * half)) >= half
    mid = blk * 2 * half + half - 1
    return half, blk, second, mid


def _hg_constants():
    C = HG_CHUNK
    t = np.arange(C)
    j = t[None, :]
    mats = [j <= t[:, None]]
    signs, masks = [], []
    for lvl in range(HG_LEVELS):
        half, blk, second, mid = _hg_level(lvl)
        if lvl < HG_COARSE:
            signs.append(np.where(second, 1.0, -1.0)[:, None] * np.ones((1, HG_D)))
        else:
            a_q = (j > mid[:, None]) & (j <= t[:, None])
            a_k = (j > t[:, None]) & (j <= mid[:, None])
            mats.append(np.where(second[:, None], a_q, a_k))
        masks.append((blk[:, None] == blk[None, :]) & second[:, None] & (~second)[None, :])
    masks.append(np.eye(C, dtype=bool))
    a = np.concatenate(mats, axis=0).astype(np.float32)
    a3 = np.concatenate([a, a, a], axis=1)
    return (jnp.asarray(a3, BF16), jnp.asarray(np.stack(signs), F32),
            jnp.asarray(np.stack(masks).astype(np.float32)))


def _hg_scan_kernel(q_ref, k_ref, v_ref, lf_ref, gate_ref, a3_ref, sign_ref, mask_ref, gout_ref, o_ref,
                    qh_ref, oi_ref, st_ref):
    C = HG_CHUNK
    n_chunks = q_ref.shape[1] // C
    gout = gout_ref[...]

    G = min(HG_GROUP, n_chunks)
    assert n_chunks % G == 0
    grp = range(G)

    def group(gi, state):
        rows = [pl.ds(pl.multiple_of((gi * G + c) * C, C), C) for c in grp]
        lf3 = []
        for c in grp:
            lf = lf_ref[0, rows[c], :]
            hi = lf.astype(BF16)
            r1 = lf - hi.astype(F32)
            mid = r1.astype(BF16)
            lo = (r1 - mid.astype(F32)).astype(BF16)
            lf3.append(jnp.concatenate([hi, mid, lo], axis=0))
        seg = _dot(a3_ref[...], jnp.concatenate(lf3, axis=1))
        seg = [seg[:, c * HG_D:(c + 1) * HG_D] for c in grp]
        b = [seg[c][0:C] for c in grp]
        q = [q_ref[0, rows[c], :].astype(F32) for c in grp]
        k = [k_ref[0, rows[c], :].astype(F32) for c in grp]
        v = [v_ref[0, rows[c], :] for c in grp]
        attn = [_dot_nt(q[c].astype(BF16), k[c].astype(BF16)) * mask_ref[HG_LEVELS] for c in grp]
        upd = [lax.dot_general(v[c], (k[c] * jnp.exp2(b[c][C - 1:C] - b[c])).astype(BF16),
                               (((0,), (0,)), ((), ())), preferred_element_type=F32) for c in grp]
        for c in grp:
            qh_ref[rows[c], :] = (q[c] * jnp.exp2(b[c])).astype(BF16)
        for lvl in range(HG_LEVELS):
            for c in grp:
                if lvl < HG_COARSE:
                    half, _, _, mids = _hg_level(lvl)
                    b_mid = jnp.concatenate([jnp.broadcast_to(b[c][m:m + 1], (2 * half, HG_D))
                                             for m in mids[::2 * half]], axis=0)
                    e = (b[c] - b_mid) * sign_ref[lvl]
                else:
                    e = seg[c][(1 + lvl - HG_COARSE) * C:(2 + lvl - HG_COARSE) * C]
                xl = jnp.exp2(e)
                attn[c] = attn[c] + _dot_nt((q[c] * xl).astype(BF16), (k[c] * xl).astype(BF16)) * mask_ref[lvl]
        for c in grp:
            oi_ref[rows[c], :] = _dot(attn[c].astype(BF16), v[c])
            st_ref[gi * G + c] = state.astype(BF16)
            state = state * jnp.exp2(b[c][C - 1:C]) + upd[c]
        return state

    lax.fori_loop(0, n_chunks // G, group, jnp.zeros((HG_D, HG_D), F32))

    GO = min(HG_OUT_GROUP, n_chunks)
    assert n_chunks % GO == 0
    out_grp = range(GO)

    def out_group(gi, carry):
        rows = [pl.ds(pl.multiple_of((gi * GO + c) * C, C), C) for c in out_grp]
        o = [_dot_nt(qh_ref[rows[c], :], st_ref[gi * GO + c]) + oi_ref[rows[c], :] for c in out_grp]
        o = [o[c] * lax.rsqrt(jnp.mean(o[c] * o[c], axis=-1, keepdims=True) + EPS) * gout for c in out_grp]
        for c in out_grp:
            o_ref[0, rows[c], :] = (o[c] * gate_ref[0, rows[c], :].astype(F32)).astype(BF16)
        return carry

    lax.fori_loop(0, n_chunks // GO, out_group, 0)


def _hg_scan(q, k, v, lf, gate, gout):
    B, S, _ = q.shape
    a3, signs, masks = _hg_constants()
    blk = pl.BlockSpec((1, S, HG_D), lambda b, h: (b, 0, h))
    return pl.pallas_call(
        _hg_scan_kernel,
        grid=(B, HG_HEADS),
        in_specs=[blk, blk, blk, blk, blk, _const_spec(a3.shape), _const_spec(signs.shape),
                  _const_spec(masks.shape), _const_spec((1, HG_D))],
        out_specs=blk,
        out_shape=jax.ShapeDtypeStruct((B, S, HG_HEADS * HG_D), BF16),
        scratch_shapes=[pltpu.VMEM((S, HG_D), BF16),
                        pltpu.VMEM((S, HG_D), F32),
                        pltpu.VMEM((S // HG_CHUNK, HG_D, HG_D), BF16)],
        compiler_params=_cparams("parallel", "parallel"),
        name="hgrn_scan",
    )(q, k, v, lf, gate, a3, signs, masks, gout)


def _cv_glu_kernel(x_ref, g_ref, w_ref, b_ref, u_ref):
    h = _rms(x_ref[...], g_ref[...]).astype(BF16)
    a = _dot(h, w_ref[:, :D_MODEL]) + b_ref[:, :D_MODEL]
    gate = _dot(h, w_ref[:, D_MODEL:]) + b_ref[:, D_MODEL:]
    u_ref[...] = a * jax.nn.sigmoid(gate)


def _cv_glu(x2, g, w, b):
    T = x2.shape[0]
    tm = TM_PROJ
    return pl.pallas_call(
        _cv_glu_kernel,
        grid=(T // tm,),
        in_specs=[_row_spec(tm, D_MODEL), _const_spec((1, D_MODEL)), _resident_spec(w.shape),
                  _const_spec((1, 2 * D_MODEL))],
        out_specs=_row_spec(tm, D_MODEL),
        out_shape=jax.ShapeDtypeStruct((T, D_MODEL), F32),
        compiler_params=_cparams("parallel"),
        name="conv_glu",
    )(x2, g, w, b)


def _conv_lane_tile(ct, tm, buf_ref, sh_ref, acc_ref, w_ref, b_ref):
    off = CONV_HALO - (CONV_WIDTH - 1)
    rb = CONV_ROWS
    cols = slice(ct * LANES, (ct + 1) * LANES)
    for j0 in range(0, CONV_WIDTH, CONV_TAP_GROUP):
        group = range(j0, min(j0 + CONV_TAP_GROUP, CONV_WIDTH))
        taps = {j: w_ref[j:j + 1, cols] for j in group}
        for base in range(0, tm, rb):
            acc = jnp.zeros((rb, LANES), F32) + b_ref[:, cols] if j0 == 0 else acc_ref[base:base + rb, cols]
            for j in group:
                a, r = divmod(off + j, SUBLANES)
                lo = base + a * SUBLANES
                src = buf_ref[lo:lo + rb, cols] if r == 0 else sh_ref[r - 1, lo:lo + rb, cols]
                acc = acc + src * taps[j]
            acc_ref[base:base + rb, cols] = acc


def _ln_swish(acc, g, b):
    mu = jnp.mean(acc, axis=-1, keepdims=True)
    xc = acc - mu
    y = xc * lax.rsqrt(jnp.mean(xc * xc, axis=-1, keepdims=True) + EPS) * g + b
    return (y * jax.nn.sigmoid(y)).astype(BF16)


def _conv_tail_kernel(u_ref, prev_ref, x_ref, w_ref, b_ref, lng_ref, lnb_ref, wo_ref, bo_ref, g_ref, win_ref,
                      wout_ref, o_ref, buf_ref, sh_ref, acc_ref, m_ref, *, tm, n_tiles, tiles_per_seq):
    i = pl.program_id(0)

    @pl.when(i == 0)
    def _():
        m_ref[...] = jnp.zeros_like(m_ref)

    tile = jnp.minimum(i, n_tiles - 1)
    buf_ref[0:CONV_HALO, :] = jnp.where(tile % tiles_per_seq == 0, 0.0, prev_ref[...])
    buf_ref[CONV_HALO:, :] = u_ref[...]
    n_sh = sh_ref.shape[1]
    for r in range(1, SUBLANES):
        sh_ref[r - 1] = buf_ref[r:r + n_sh, :]

    x1 = x_ref[...] + _dot(m_ref[(i + 1) % 2], wo_ref[...]) + bo_ref[...]
    h = _rms(x1, g_ref[...]).astype(BF16)
    out = x1
    n_ff = D_FF // TF_TAIL
    n_ct = D_MODEL // LANES
    for kk in range(n_ff):
        a = jnp.maximum(_dot(h, win_ref[:, kk * TF_TAIL:(kk + 1) * TF_TAIL]), 0.0)
        out = out + _dot((a * a).astype(BF16), wout_ref[kk * TF_TAIL:(kk + 1) * TF_TAIL, :])
        for ct in range(kk * n_ct // n_ff, (kk + 1) * n_ct // n_ff):
            _conv_lane_tile(ct, tm, buf_ref, sh_ref, acc_ref, w_ref, b_ref)
    o_ref[...] = out
    m_ref[i % 2] = _ln_swish(acc_ref[...], lng_ref[...], lnb_ref[...])


def _conv_tail(u, x2, seq_len, w, b, lng, lnb, wo, bo, g, win, wout):
    T = x2.shape[0]
    tm = TM_CONV
    n_tiles = T // tm
    ratio = tm // CONV_HALO
    last = n_tiles - 1
    return pl.pallas_call(
        functools.partial(_conv_tail_kernel, tm=tm, n_tiles=n_tiles, tiles_per_seq=seq_len // tm),
        grid=(n_tiles + 1,),
        in_specs=[pl.BlockSpec((tm, D_MODEL), lambda i: (jnp.minimum(i, last), 0)),
                  pl.BlockSpec((CONV_HALO, D_MODEL), lambda i: (jnp.maximum(jnp.minimum(i, last) * ratio - 1, 0), 0)),
                  pl.BlockSpec((tm, D_MODEL), lambda i: (jnp.maximum(i - 1, 0), 0)),
                  _const_spec((CONV_WIDTH, D_MODEL)), _const_spec((1, D_MODEL)),
                  _const_spec((1, D_MODEL)), _const_spec((1, D_MODEL)),
                  _resident_spec(wo.shape), _const_spec((1, D_MODEL)), _const_spec((1, D_MODEL)),
                  _resident_spec(win.shape), _resident_spec(wout.shape)],
        out_specs=pl.BlockSpec((tm, D_MODEL), lambda i: (jnp.maximum(i - 1, 0), 0)),
        out_shape=jax.ShapeDtypeStruct((T, D_MODEL), F32),
        scratch_shapes=[pltpu.VMEM((tm + CONV_HALO, D_MODEL), F32),
                        pltpu.VMEM((SUBLANES - 1, tm + CONV_HALO - SUBLANES, D_MODEL), F32),
                        pltpu.VMEM((tm, D_MODEL), F32),
                        pltpu.VMEM((2, tm, D_MODEL), BF16)],
        compiler_params=_cparams("arbitrary"),
        name="conv_tail",
    )(u, u, x2, w, b, lng, lnb, wo, bo, g, win, wout)


def _dup_rope(t):
    return jnp.concatenate([t, t], axis=-1)


def _mla_weights(w_down, w_uq, q_head_norm, k_head_norm):
    wd = jnp.concatenate([w_down, w_down[:, -MLA_ROPE:]], axis=1).astype(BF16)
    wq = w_uq.reshape(MLA_Q_LORA, MLA_HEADS, MLA_QK)
    wq = jnp.concatenate([wq, wq[:, :, MLA_NOPE:]], axis=-1)
    wq = wq.reshape(MLA_Q_LORA, MLA_HEADS * HEAD_PAD).astype(BF16)
    qng = q_head_norm[None, :MLA_NOPE]
    qrg = _dup_rope(q_head_norm[None, MLA_NOPE:])
    kng = k_head_norm[None, :MLA_NOPE]
    krg = _dup_rope(k_head_norm[None, MLA_NOPE:])
    return wd, wq, qng, qrg, kng, krg


def kernel(x, positions, norm_mix, norm_mlp, mlp_w_in, mlp_w_out, mla_w_down, mla_q_lat_norm, mla_kv_lat_norm, mla_w_uq, mla_w_ukv, mla_q_head_norm, mla_k_head_norm, mla_w_o, hg_w_in, hg_lb_logits, hg_out_norm, hg_w_o, cv_w_pw1, cv_b_pw1, cv_w_dw, cv_b_dw, cv_ln_g, cv_ln_b, cv_w_pw2, cv_b_pw2):
    B, S, D = x.shape
    T = B * S
    depth = norm_mix.shape[0]
    x2 = x.reshape(T, D)

    inv_freq = jnp.power(ROPE_THETA, -jnp.arange(0, MLA_ROPE, 2, dtype=F32) / MLA_ROPE)
    invf4 = jnp.tile(inv_freq, 4)[None, :]
    cos, sin = _rope_tables(positions.astype(F32).reshape(T, 1), invf4)
    lb_table = _hg_lb_table(hg_lb_logits.astype(F32))
    zero_bias = jnp.zeros((1, D), F32)

    for layer in range(depth):
        kind, j = layer % N_MIXERS, layer // N_MIXERS
        g_mix = norm_mix[layer][None, :]
        if kind == 0:
            wd, wq, qng, qrg, kng, krg = _mla_weights(mla_w_down[j], mla_w_uq[j], mla_q_head_norm[j],
                                                      mla_k_head_norm[j])
            q, kn, kr, v = _mla_proj(x2, g_mix, wd, mla_q_lat_norm[j][None, :], mla_kv_lat_norm[j][None, :],
                                     wq, mla_w_ukv[j].astype(BF16), qng, qrg, kng, krg, cos, sin)
            m = _attention(q.reshape(B, S, -1), kn.reshape(B, S, -1), kr.reshape(B, S, -1),
                           v.reshape(B, S, -1)).reshape(T, D)
            w_o, b_o = mla_w_o[j], zero_bias
        elif kind == 1:
            q, k, v, lf, gate = _hg_proj(x2, g_mix, hg_w_in[j].astype(BF16), lb_table[layer][None, :])
            r3 = lambda t: t.reshape(B, S, -1)
            m = _hg_scan(r3(q), r3(k), r3(v), r3(lf), r3(gate), hg_out_norm[j][None, :]).reshape(T, D)
            w_o, b_o = hg_w_o[j], zero_bias
        else:
            u = _cv_glu(x2, g_mix, cv_w_pw1[j].astype(BF16), cv_b_pw1[j][None, :])
            x2 = _conv_tail(u, x2, S, cv_w_dw[j], cv_b_dw[j][None, :], cv_ln_g[j][None, :], cv_ln_b[j][None, :],
                            cv_w_pw2[j].astype(BF16), cv_b_pw2[j][None, :], norm_mlp[layer][None, :],
                            mlp_w_in[layer].astype(BF16), mlp_w_out[layer].astype(BF16))
            continue
        x2 = _tail(x2, m, w_o.astype(BF16), b_o, norm_mlp[layer][None, :],
                   mlp_w_in[layer].astype(BF16), mlp_w_out[layer].astype(BF16))
    return x2.reshape(B, S, D)
```
